```python
import functools
import jax, jax.numpy as jnp
from jax import lax
import numpy as np

D_MODEL = 2048
BATCH = 4
SEQ = 2048
DEPTH = 1
DEC_BATCH = 128
DEC_SEQ = 8
PAST_LEN = 8192
PAGE_SIZE = 128

A_HEAD_DIM = 128
A_HEADS = (D_MODEL // 2) // A_HEAD_DIM
A_KV_HEADS = A_HEADS // 2
A_GROUP = A_HEADS // A_KV_HEADS
A_WIDTH = A_HEADS * A_HEAD_DIM
A_KV_WIDTH = A_KV_HEADS * A_HEAD_DIM
A_BLOCK = 256
A_TOPK = 3
A_Q_CHUNK = 64
A_CHUNKS_PER_BLOCK = A_BLOCK // A_Q_CHUNK
A_SCALE = A_HEAD_DIM ** -0.5
B_NOPE = 128
B_ROPE = 64
B_VDIM = 128
B_HEADS = (D_MODEL // 2) // B_VDIM
B_WIDTH = B_HEADS * B_VDIM
B_Q_LORA = 512
B_KV_LORA = 256
B_SCALE = (B_NOPE + B_ROPE) ** -0.5
B_Q_BLOCK = 128
MIX_WIDTH = A_WIDTH + B_WIDTH
IN_SIZES = (A_WIDTH, A_KV_WIDTH, A_KV_WIDTH, A_WIDTH, B_Q_LORA, B_KV_LORA, B_ROPE, B_WIDTH)
IN_SPLITS = tuple(int(s) for s in np.cumsum(IN_SIZES)[:-1])
IN_TOTAL = int(sum(IN_SIZES))
ROPE_THETA = 10000.0
RMS_EPS = 1e-6
LN_EPS = 1e-5
ALPHA = (2 * DEPTH) ** 0.25
BETA = (8 * DEPTH) ** -0.25

kernel_name = 'moba_mla_hybrid_step'


def rope(x, pos):
    half = x.shape[-1] // 2
    inv = jnp.power(ROPE_THETA, -jnp.arange(half, dtype=jnp.float32) / half)
    ang = pos.astype(jnp.float32)[:, None] * inv[None, :]
    cos = jnp.cos(ang)[:, None, :]
    sin = jnp.sin(ang)[:, None, :]
    xf = x.astype(jnp.float32)
    x1, x2 = xf[..., :half], xf[..., half:]
    return jnp.concatenate([x1 * cos - x2 * sin, x2 * cos + x1 * sin], axis=-1).astype(x.dtype)


def rms_norm(x, g):
    xf = x.astype(jnp.float32)
    return (xf * lax.rsqrt(jnp.mean(xf * xf, axis=-1, keepdims=True) + RMS_EPS)).astype(x.dtype) * g


def layer_norm(x, g, b):
    xf = x.astype(jnp.float32)
    mu = jnp.mean(xf, axis=-1, keepdims=True)
    var = jnp.mean(jnp.square(xf - mu), axis=-1, keepdims=True)
    return ((xf - mu) * lax.rsqrt(var + LN_EPS)).astype(x.dtype) * g + b


def moba_select(q, k_means, block_ok, n_sel):
    nb_, nq, nh, d = q.shape
    s = jnp.einsum('bqkgd,bnkd->bkgqn', q.reshape(nb_, nq, A_KV_HEADS, A_GROUP, d), k_means)
    s = s.reshape(nb_, nh, nq, -1).astype(jnp.float32)
    if block_ok is not None:
        s = jnp.where(block_ok, s, -jnp.inf)
    return lax.top_k(s, n_sel)[1]


def moba_attend(q, own_k, own_v, own_mask, sel_k=None, sel_v=None, sel_ok=None):
    nb_, nq, nh, d = q.shape
    r = own_k.shape[1]
    qg = q.reshape(nb_, nq, A_KV_HEADS, A_GROUP, d)
    s_own = jnp.einsum('bqkgd,brkd->bkgqr', qg, own_k).reshape(nb_, nh, nq, r).astype(jnp.float32) * A_SCALE
    s_own = jnp.where(own_mask, s_own, -jnp.inf)
    if sel_k is None:
        p_own = jax.nn.softmax(s_own, axis=-1)
        sel_out = 0.0
    else:
        n_sel = sel_k.shape[3]
        s_sel = jnp.einsum('bqhd,bhqknd->bhqkn', q, sel_k).astype(jnp.float32) * A_SCALE
        if sel_ok is not None:
            s_sel = jnp.where(sel_ok[..., None], s_sel, -jnp.inf)
        p = jax.nn.softmax(jnp.concatenate([s_sel.reshape(nb_, nh, nq, n_sel * A_BLOCK), s_own], axis=-1), axis=-1)
        p_sel = p[..., :n_sel * A_BLOCK].reshape(nb_, nh, nq, n_sel, A_BLOCK).astype(q.dtype)
        p_own = p[..., n_sel * A_BLOCK:]
        sel_out = jnp.einsum('bhqkn,bhqknd->bqhd', p_sel, sel_v)
    p_own = p_own.astype(q.dtype).reshape(nb_, A_KV_HEADS, A_GROUP, nq, r)
    own_out = jnp.einsum('bkgqr,brkd->bqkgd', p_own, own_v).reshape(nb_, nq, nh, d)
    return (own_out + sel_out).reshape(nb_, nq, nh * d)


def moba_prompt(q, k, v):
    nb_, s_len, nh, d = q.shape
    n_blk = -(-s_len // A_BLOCK)
    s_pad = n_blk * A_BLOCK
    pad = ((0, 0), (0, s_pad - s_len), (0, 0), (0, 0))
    kb = jnp.pad(k, pad).reshape(nb_, n_blk, A_BLOCK, A_KV_HEADS, d)
    vb = jnp.pad(v, pad).reshape(nb_, n_blk, A_BLOCK, A_KV_HEADS, d)
    k_means = jnp.mean(kb.astype(jnp.float32), axis=2).astype(k.dtype)
    n_sel = min(A_TOPK, n_blk - 1)
    n_chunks = s_pad // A_Q_CHUNK
    qc = jnp.pad(q, pad).reshape(nb_, n_chunks, A_Q_CHUNK, nh, d).swapaxes(0, 1)
    bidx = jnp.arange(nb_)[:, None, None, None]
    hkv = (jnp.arange(nh) // A_GROUP)[None, :, None, None]

    def one_chunk(args):
        ci, qi = args
        blk = ci // A_CHUNKS_PER_BLOCK
        off = (ci % A_CHUNKS_PER_BLOCK) * A_Q_CHUNK
        own_mask = (off + jnp.arange(A_Q_CHUNK))[:, None] >= jnp.arange(A_BLOCK)[None, :]
        own_k = lax.dynamic_index_in_dim(kb, blk, axis=1, keepdims=False)
        own_v = lax.dynamic_index_in_dim(vb, blk, axis=1, keepdims=False)
        if n_sel == 0:
            return moba_attend(qi, own_k, own_v, own_mask)
        idx = moba_select(qi, k_means, jnp.arange(n_blk) < blk, n_sel)
        sel_k = kb[bidx, idx, :, hkv]
        sel_v = vb[bidx, idx, :, hkv]
        return moba_attend(qi, own_k, own_v, own_mask, sel_k, sel_v, idx < blk)

    out = lax.map(one_chunk, (jnp.arange(n_chunks), qc))
    return out.swapaxes(0, 1).reshape(nb_, s_pad, nh * d)[:, :s_len]


def moba_sample(q, k_new, v_new, cache_k, cache_v, page_table, layer):
    n_t = q.shape[1]
    nh, d = A_HEADS, A_HEAD_DIM
    n_full = PAST_LEN // A_BLOCK
    tail = PAST_LEN - n_full * A_BLOCK
    n_sel = min(A_TOPK, n_full)
    ppb = A_BLOCK // PAGE_SIZE
    own_mask = jnp.arange(tail + n_t)[None, :] <= tail + jnp.arange(n_t)[:, None]
    hkv = (jnp.arange(nh) // A_GROUP)[:, None, None, None]

    def one_seq(args):
        qs, ks, vs, pt = args
        own_k, own_v = ks, vs
        if tail > 0:
            own_k = jnp.concatenate([cache_k[layer, pt[n_full * ppb:]].reshape(tail, A_KV_HEADS, d), ks], axis=0)
            own_v = jnp.concatenate([cache_v[layer, pt[n_full * ppb:]].reshape(tail, A_KV_HEADS, d), vs], axis=0)
        if n_sel == 0:
            return moba_attend(qs[None], own_k[None], own_v[None], own_mask)[0]
        k_full = cache_k[layer, pt[:n_full * ppb]].reshape(n_full, A_BLOCK, A_KV_HEADS, d)
        k_means = jnp.mean(k_full.astype(jnp.float32), axis=1).astype(qs.dtype)
        idx = moba_select(qs[None], k_means[None], None, n_sel)[0]
        pages = pt[idx[..., None] * ppb + jnp.arange(ppb)]
        sel_k = cache_k[layer, pages, :, hkv].reshape(nh, n_t, n_sel, A_BLOCK, d)
        sel_v = cache_v[layer, pages, :, hkv].reshape(nh, n_t, n_sel, A_BLOCK, d)
        return moba_attend(qs[None], own_k[None], own_v[None], own_mask, sel_k[None], sel_v[None])[0]

    return lax.map(one_seq, (q, k_new, v_new, page_table))


def mla_attend(q_lat, q_rope, ckv, kr, mask):
    s = jnp.einsum('bthc,bsc->bhts', q_lat, ckv) + jnp.einsum('bthr,bsr->bhts', q_rope, kr)
    s = jnp.where(mask, s.astype(jnp.float32) * B_SCALE, -jnp.inf)
    p = jax.nn.softmax(s, axis=-1).astype(ckv.dtype)
    return jnp.einsum('bhts,bsc->bthc', p, ckv)


def mla_prompt(q_lat, q_rope, ckv, kr):
    nb_, s_len, nh, c = q_lat.shape
    nq = s_len // B_Q_BLOCK
    ql = q_lat.reshape(nb_, nq, B_Q_BLOCK, nh, c).swapaxes(0, 1)
    qr = q_rope.reshape(nb_, nq, B_Q_BLOCK, nh, B_ROPE).swapaxes(0, 1)
    kpos = jnp.arange(s_len)

    def one_block(args):
        i, qli, qri = args
        mask = (i * B_Q_BLOCK + jnp.arange(B_Q_BLOCK))[:, None] >= kpos[None, :]
        return mla_attend(qli, qri, ckv, kr, mask)

    out = lax.map(one_block, (jnp.arange(nq), ql, qr))
    return out.swapaxes(0, 1).reshape(nb_, s_len, nh, c)


def mla_sample(q_lat, q_rope, ckv, kr, cache_ckv, cache_kr, page_table, layer):
    n_db, n_t = q_lat.shape[:2]
    ckv_all = jnp.concatenate([cache_ckv[layer, page_table].reshape(n_db, PAST_LEN, B_KV_LORA), ckv], axis=1)
    kr_all = jnp.concatenate([cache_kr[layer, page_table].reshape(n_db, PAST_LEN, B_ROPE), kr], axis=1)
    mask = jnp.arange(PAST_LEN + n_t)[None, :] <= PAST_LEN + jnp.arange(n_t)[:, None]
    return mla_attend(q_lat, q_rope, ckv_all, kr_all, mask)


def mixer_layer(x, c, pos, attend_a, attend_b, w_ada, b_ada, w_in, q_norm_g, w_uq,
                kv_norm_g, w_uk, w_uv, w_out, ln_g, ln_b):
    nb_, n_t, _ = x.shape
    mod = jax.nn.silu(c) @ w_ada + b_ada
    shift, scale, gate = jnp.split(mod[:, None, :], 3, axis=-1)
    h = x * (1.0 + scale) + shift
    z = h @ w_in
    qa, ka, va, ga, cq, ckv, kr, gb = jnp.split(z, IN_SPLITS, axis=-1)
    qa = rope(qa.reshape(nb_, n_t, A_HEADS, A_HEAD_DIM), pos)
    ka = rope(ka.reshape(nb_, n_t, A_KV_HEADS, A_HEAD_DIM), pos)
    va = va.reshape(nb_, n_t, A_KV_HEADS, A_HEAD_DIM)
    qb = (rms_norm(cq, q_norm_g) @ w_uq).reshape(nb_, n_t, B_HEADS, B_NOPE + B_ROPE)
    q_lat = jnp.einsum('bthn,chn->bthc', qb[..., :B_NOPE], w_uk)
    q_rope = rope(qb[..., B_NOPE:], pos)
    ckv = rms_norm(ckv, kv_norm_g)
    kr = rope(kr[:, :, None, :], pos)[:, :, 0]
    oa = attend_a(qa, ka, va)
    ob = jnp.einsum('bthc,chv->bthv', attend_b(q_lat, q_rope, ckv, kr), w_uv).reshape(nb_, n_t, B_WIDTH)
    o = jnp.concatenate([oa * jax.nn.silu(ga), ob * jax.nn.silu(gb)], axis=-1) @ w_out
    y = layer_norm(ALPHA * x + gate * o, ln_g, ln_b)
    return y, (ka, va, ckv, kr)


def setup_inputs(seed: int = 0) -> dict:
    key = jax.random.key(seed)
    ks = jax.random.split(key, 20)
    n_pages = PAST_LEN // PAGE_SIZE
    n_used = DEC_BATCH * n_pages
    n_pool = n_used + (n_used + 3) // 4

    def nrm(k, shape, s=1.0):
        return jax.random.normal(k, shape, jnp.float32) * s

    page_table = jax.random.permutation(ks[6], n_pool)[:n_used].reshape(DEC_BATCH, n_pages).astype(jnp.int32)
    return {
        'x_prompt': nrm(ks[0], (BATCH, SEQ, D_MODEL)),
        'x_sample': nrm(ks[1], (DEC_BATCH, DEC_SEQ, D_MODEL)),
        'cache_k': nrm(ks[2], (DEPTH, n_pool, PAGE_SIZE, A_KV_HEADS, A_HEAD_DIM)),
        'cache_v': nrm(ks[3], (DEPTH, n_pool, PAGE_SIZE, A_KV_HEADS, A_HEAD_DIM)),
        'cache_ckv': nrm(ks[4], (DEPTH, n_pool, PAGE_SIZE, B_KV_LORA)),
        'cache_kr': nrm(ks[5], (DEPTH, n_pool, PAGE_SIZE, B_ROPE)),
        'page_table': page_table,
        'c_prompt': nrm(ks[7], (BATCH, D_MODEL)),
        'c_sample': nrm(ks[8], (DEC_BATCH, D_MODEL)),
        'w_ada': nrm(ks[9], (DEPTH, D_MODEL, 3 * D_MODEL), 0.5 * D_MODEL ** -0.5),
        'b_ada': nrm(ks[10], (DEPTH, 3 * D_MODEL), 0.01),
        'w_in': nrm(ks[11], (DEPTH, D_MODEL, IN_TOTAL), D_MODEL ** -0.5),
        'q_norm_g': 1.0 + nrm(ks[12], (DEPTH, B_Q_LORA), 0.02),
        'w_uq': nrm(ks[13], (DEPTH, B_Q_LORA, B_HEADS * (B_NOPE + B_ROPE)), B_Q_LORA ** -0.5),
        'kv_norm_g': 1.0 + nrm(ks[14], (DEPTH, B_KV_LORA), 0.02),
        'w_uk': nrm(ks[15], (DEPTH, B_KV_LORA, B_HEADS, B_NOPE), B_KV_LORA ** -0.5),
        'w_uv': nrm(ks[16], (DEPTH, B_KV_LORA, B_HEADS, B_VDIM), B_KV_LORA ** -0.5),
        'w_out': nrm(ks[17], (DEPTH, MIX_WIDTH, D_MODEL), BETA * MIX_WIDTH ** -0.5),
        'ln_g': 1.0 + nrm(ks[18], (DEPTH, D_MODEL), 0.02),
        'ln_b': nrm(ks[19], (DEPTH, D_MODEL), 0.02),
    }


def reference(x_prompt, x_sample, cache_k, cache_v, cache_ckv, cache_kr, page_table, c_prompt, c_sample,
              w_ada, b_ada, w_in, q_norm_g, w_uq, kv_norm_g, w_uk, w_uv, w_out, ln_g, ln_b):
    pos_p = jnp.arange(x_prompt.shape[1], dtype=jnp.int32)
    pos_s = PAST_LEN + jnp.arange(x_sample.shape[1], dtype=jnp.int32)
    y_prompt, y_sample = x_prompt, x_sample
    new_p, new_s = [], []
    for layer in range(DEPTH):
        w = (w_ada[layer], b_ada[layer], w_in[layer], q_norm_g[layer], w_uq[layer], kv_norm_g[layer],
             w_uk[layer], w_uv[layer], w_out[layer], ln_g[layer], ln_b[layer])
        y_prompt, st_p = mixer_layer(y_prompt, c_prompt, pos_p, moba_prompt, mla_prompt, *w)
        attend_a = functools.partial(moba_sample, cache_k=cache_k, cache_v=cache_v,
                                     page_table=page_table, layer=layer)
        attend_b = functools.partial(mla_sample, cache_ckv=cache_ckv, cache_kr=cache_kr,
                                     page_table=page_table, layer=layer)
        y_sample, st_s = mixer_layer(y_sample, c_sample, pos_s, attend_a, attend_b, *w)
        new_p.append(st_p)
        new_s.append(st_s)
    k_prompt, v_prompt, ckv_prompt, kr_prompt = [jnp.stack(a) for a in zip(*new_p)]
    k_sample, v_sample, ckv_sample, kr_sample = [jnp.stack(a) for a in zip(*new_s)]
    return (y_prompt, y_sample, k_prompt, v_prompt, ckv_prompt, kr_prompt,
            k_sample, v_sample, ckv_sample, kr_sample)
```

```python
import functools

import jax
import jax.numpy as jnp
from jax import lax
from jax.experimental import pallas as pl
from jax.experimental.pallas import tpu as pltpu

F32 = jnp.float32
BF16 = jnp.bfloat16
NEG_INF = float("-inf")

D_MODEL = 2048
BATCH = 4
SEQ = 2048
DEC_BATCH = 128
DEC_SEQ = 8
PAST_LEN = 8192
PAGE_SIZE = 128
N_PAGES = PAST_LEN // PAGE_SIZE

A_HEAD_DIM = 128
A_HEADS = 8
A_KV_HEADS = 4
A_GROUP = A_HEADS // A_KV_HEADS
A_WIDTH = A_HEADS * A_HEAD_DIM
A_KV_WIDTH = A_KV_HEADS * A_HEAD_DIM
A_BLOCK = 256
A_TOPK = 3
A_SCALE = A_HEAD_DIM ** -0.5
B_NOPE = 128
B_ROPE = 64
B_VDIM = 128
B_HEADS = 8
B_WIDTH = B_HEADS * B_VDIM
B_Q_LORA = 512
B_KV_LORA = 256
B_SCALE = (B_NOPE + B_ROPE) ** -0.5
ROPE_THETA = 10000.0
RMS_EPS = 1e-6
LN_EPS = 1e-5
DEPTH = 1
ALPHA = (2 * DEPTH) ** 0.25

N_BLOCKS_PROMPT = SEQ // A_BLOCK
N_BLOCKS_PAST = PAST_LEN // A_BLOCK
PAGES_PER_BLOCK = A_BLOCK // PAGE_SIZE

C_QA = (0, 1024)
C_KA = (1024, 1536)
C_VA = (1536, 2048)
C_GA = (2048, 3072)
C_CQ = (3072, 3584)
C_CKV = (3584, 3840)
C_GB = (3840, 4864)
C_KR = (4864, 4992)
W_IN_COLS = 4992

LANES = 128
VMEM_LIMIT_BYTES = 56 * 1024 * 1024


def _params(semantics):
    return pltpu.CompilerParams(dimension_semantics=semantics, vmem_limit_bytes=VMEM_LIMIT_BYTES)


def _silu(x):
    return x * jax.nn.sigmoid(x)


def _dot(a, b):
    return jnp.dot(a, b, preferred_element_type=F32)


def _dot_nt(a, b):
    return lax.dot_general(a, b, (((1,), (1,)), ((), ())), preferred_element_type=F32)


def _resident(shape):
    nd = len(shape)
    return pl.BlockSpec(shape, lambda *_: (0,) * nd, pipeline_mode=pl.Buffered(1))


def _ada_kernel(c_ref, w_ref, b_ref, o_ref):
    a = _silu(c_ref[...]).astype(BF16)
    o_ref[...] = _dot(a, w_ref[...].astype(BF16)) + b_ref[...]


def _ada_mod(c_all, w_ada, b_ada):
    rows = c_all.shape[0]
    n = w_ada.shape[1]
    tn = 768
    return pl.pallas_call(
        _ada_kernel,
        grid=(n // tn,),
        in_specs=[pl.BlockSpec((rows, D_MODEL), lambda j: (0, 0)),
                  pl.BlockSpec((D_MODEL, tn), lambda j: (0, j)),
                  pl.BlockSpec((1, tn), lambda j: (0, j))],
        out_specs=pl.BlockSpec((rows, tn), lambda j: (0, j)),
        out_shape=jax.ShapeDtypeStruct((rows, n), F32),
        compiler_params=_params(("arbitrary",)),
        name="ada_mod",
    )(c_all, w_ada, b_ada)


def _in_proj_kernel(x_ref, shift_ref, scale_ref, w_ref, wuqn_ref, wuqr_ref, wuk_ref, qg_ref, kvg_ref,
                    cosa_ref, sina_ref, cosb_ref, sinb_ref,
                    qa_ref, ka_ref, va_ref, ga_ref, qlat_ref, qrope_ref, ckv_ref, kr_ref, gb_ref,
                    *, per_row_mod):
    x = x_ref[...]
    if per_row_mod:
        shift, scale = shift_ref[...], scale_ref[...]
    else:
        shift, scale = shift_ref[0], scale_ref[0]
    hb = (x * (1.0 + scale) + shift).astype(BF16)
    tm = x.shape[0]
    cosa, sina = cosa_ref[...], sina_ref[...]
    cosb, sinb = cosb_ref[...], sinb_ref[...]

    def proj(cols):
        return _dot(hb, w_ref[:, cols[0]:cols[1]])

    def rope_full(z, n_heads):
        outs = []
        for h in range(n_heads):
            zh = z[:, h * LANES:(h + 1) * LANES]
            outs.append(zh * cosa + pltpu.roll(zh, 64, 1) * sina)
        return jnp.concatenate(outs, axis=1)

    lane = lax.broadcasted_iota(jnp.int32, (tm, LANES), 1)
    first_half = (lane & 63) < 32

    def rope_half(zg):
        partner = jnp.where(first_half, pltpu.roll(zg, 96, 1), pltpu.roll(zg, 32, 1))
        return zg * cosb + partner * sinb

    def rms(z, g):
        return z * lax.rsqrt(jnp.mean(z * z, axis=-1, keepdims=True) + RMS_EPS) * g

    qa_ref[...] = rope_full(proj(C_QA), A_HEADS)
    ka_ref[...] = rope_full(proj(C_KA), A_KV_HEADS)
    va_ref[...] = proj(C_VA)
    ga_ref[...] = proj(C_GA)
    gb_ref[...] = proj(C_GB)
    ckv_ref[...] = rms(proj(C_CKV), kvg_ref[...])
    kr_ref[...] = rope_half(proj(C_KR))[:, :B_ROPE]

    cqn = rms(proj(C_CQ), qg_ref[...]).astype(BF16)
    qn = _dot(cqn, wuqn_ref[...])
    qr = _dot(cqn, wuqr_ref[...])
    for h in range(B_HEADS):
        qlat_ref[:, h * B_KV_LORA:(h + 1) * B_KV_LORA] = _dot(
            qn[:, h * B_NOPE:(h + 1) * B_NOPE].astype(BF16), wuk_ref[h])
    qrope_ref[...] = jnp.concatenate(
        [rope_half(qr[:, g * LANES:(g + 1) * LANES]) for g in range(B_HEADS * B_ROPE // LANES)], axis=1)


def _in_proj(x, mod, w_p, wuq_n, wuq_r, wuk_t, q_norm_g, kv_norm_g, tabs, *, per_row_mod, rows_per_batch):
    rows = x.shape[0]
    tm = 256
    steps = rows // tm
    if per_row_mod:
        shift_spec = pl.BlockSpec((tm, D_MODEL), lambda i: (i, 0))
        scale_spec = pl.BlockSpec((tm, D_MODEL), lambda i: (i, 1))
        tab_spec = pl.BlockSpec((tm, LANES), lambda i: (0, 0))
    else:
        tiles_per_batch = rows_per_batch // tm
        shift_spec = pl.BlockSpec((1, 1, D_MODEL), lambda i: (i // tiles_per_batch, 0, 0))
        scale_spec = pl.BlockSpec((1, 1, D_MODEL), lambda i: (i // tiles_per_batch, 0, 1))
        tab_spec = pl.BlockSpec((tm, LANES), lambda i: (i % tiles_per_batch, 0))
    widths = (A_WIDTH, A_KV_WIDTH, A_KV_WIDTH, A_WIDTH, B_HEADS * B_KV_LORA, B_HEADS * B_ROPE,
              B_KV_LORA, B_ROPE, B_WIDTH)
    return pl.pallas_call(
        functools.partial(_in_proj_kernel, per_row_mod=per_row_mod),
        grid=(steps,),
        in_specs=[pl.BlockSpec((tm, D_MODEL), lambda i: (i, 0)), shift_spec, scale_spec,
                  _resident(w_p.shape), _resident(wuq_n.shape), _resident(wuq_r.shape),
                  _resident(wuk_t.shape), _resident(q_norm_g.shape), _resident(kv_norm_g.shape),
                  tab_spec, tab_spec, tab_spec, tab_spec],
        out_specs=[pl.BlockSpec((tm, w), lambda i: (i, 0)) for w in widths],
        out_shape=[jax.ShapeDtypeStruct((rows, w), F32) for w in widths],
        compiler_params=_params(("arbitrary",)),
        name="in_proj_sample" if per_row_mod else "in_proj_prompt",
    )(x, mod, mod, w_p, wuq_n, wuq_r, wuk_t, q_norm_g, kv_norm_g, *tabs)


def _moba_prompt_kernel(q_ref, k_ref, v_ref, o_ref, kb_s, vb_s, km_s, m_s, l_s, acc_s):
    i = pl.program_id(2)
    rows = A_GROUP * A_BLOCK

    @pl.when(i == 0)
    def _():
        k = k_ref[0]
        kb_s[...] = k.astype(BF16)
        vb_s[...] = v_ref[0].astype(BF16)
        for j in range(N_BLOCKS_PROMPT):
            km_s[j:j + 1, :] = jnp.sum(k[j * A_BLOCK:(j + 1) * A_BLOCK], axis=0, keepdims=True) * (1.0 / A_BLOCK)

    q2 = q_ref[0]
    qb = jnp.concatenate([q2[:, g * A_HEAD_DIM:(g + 1) * A_HEAD_DIM] for g in range(A_GROUP)],
                         axis=0).astype(BF16)

    sg = _dot_nt(qb, km_s[...].astype(BF16))
    blk = lax.broadcasted_iota(jnp.int32, sg.shape, 1)
    rank = jnp.zeros(sg.shape, jnp.int32)
    for jp in range(N_BLOCKS_PROMPT - 1):
        col = sg[:, jp:jp + 1]
        beats = (col > sg) | ((col == sg) & (blk > jp))
        rank = rank + jnp.where(beats, jnp.where(jp < i, 1, 0), 0)
    sel = (blk < i) & (rank < A_TOPK)

    start = pl.multiple_of(i * A_BLOCK, A_BLOCK)
    s = _dot_nt(qb, kb_s[pl.ds(start, A_BLOCK), :]) * A_SCALE
    tok = lax.broadcasted_iota(jnp.int32, s.shape, 0) & (A_BLOCK - 1)
    key = lax.broadcasted_iota(jnp.int32, s.shape, 1)
    s = jnp.where(tok >= key, s, NEG_INF)
    m = jnp.max(s, axis=1, keepdims=True)
    p = jnp.exp(s - m)
    m_s[...] = m
    l_s[...] = jnp.sum(p, axis=1, keepdims=True)
    acc_s[...] = _dot(p.astype(BF16), vb_s[pl.ds(start, A_BLOCK), :])

    for j in range(N_BLOCKS_PROMPT - 1):
        @pl.when(j < i)
        def _(j=j):
            sj = _dot_nt(qb, kb_s[j * A_BLOCK:(j + 1) * A_BLOCK, :]) * A_SCALE
            sj = jnp.where(sel[:, j:j + 1], sj, NEG_INF)
            m_old = m_s[...]
            m_new = jnp.maximum(m_old, jnp.max(sj, axis=1, keepdims=True))
            alpha = jnp.exp(m_old - m_new)
            pj = jnp.exp(sj - m_new)
            l_s[...] = alpha * l_s[...] + jnp.sum(pj, axis=1, keepdims=True)
            acc_s[...] = alpha * acc_s[...] + _dot(pj.astype(BF16), vb_s[j * A_BLOCK:(j + 1) * A_BLOCK, :])
            m_s[...] = m_new

    out = acc_s[...] / l_s[...]
    o_ref[0] = jnp.concatenate([out[g * A_BLOCK:(g + 1) * A_BLOCK] for g in range(A_GROUP)], axis=1)


def _moba_prompt(qa, ka, va):
    rows = A_GROUP * A_BLOCK
    gw = A_GROUP * A_HEAD_DIM
    return pl.pallas_call(
        _moba_prompt_kernel,
        grid=(BATCH, A_KV_HEADS, N_BLOCKS_PROMPT),
        in_specs=[pl.BlockSpec((1, A_BLOCK, gw), lambda b, g, i: (b, i, g)),
                  pl.BlockSpec((1, SEQ, A_HEAD_DIM), lambda b, g, i: (b, 0, g)),
                  pl.BlockSpec((1, SEQ, A_HEAD_DIM), lambda b, g, i: (b, 0, g))],
        out_specs=pl.BlockSpec((1, A_BLOCK, gw), lambda b, g, i: (b, i, g)),
        out_shape=jax.ShapeDtypeStruct((BATCH, SEQ, A_WIDTH), F32),
        scratch_shapes=[pltpu.VMEM((SEQ, A_HEAD_DIM), BF16), pltpu.VMEM((SEQ, A_HEAD_DIM), BF16),
                        pltpu.VMEM((N_BLOCKS_PROMPT, A_HEAD_DIM), F32),
                        pltpu.VMEM((rows, 1), F32), pltpu.VMEM((rows, 1), F32),
                        pltpu.VMEM((rows, A_HEAD_DIM), F32)],
        compiler_params=_params(("arbitrary", "arbitrary", "arbitrary")),
        name="moba_prompt",
    )(qa, ka, va)


MLA_TQ = 128
MLA_TK = 256


def _mla_prompt_kernel(ql_ref, qr_ref, ckv_ref, kr_ref, wuv_ref, o_ref,
                       ckvb_s, krb_s, qls_s, qrs_s, m_s, l_s, acc_s):
    i = pl.program_id(1)
    tq = MLA_TQ

    @pl.when(i == 0)
    def _():
        ckvb_s[...] = ckv_ref[0].astype(BF16)
        krb_s[...] = kr_ref[0].astype(BF16)

    ql = ql_ref[0]
    qr = qr_ref[0]
    for h in range(B_HEADS):
        qls_s[h * tq:(h + 1) * tq, :] = ql[:, h * B_KV_LORA:(h + 1) * B_KV_LORA].astype(BF16)
        qrs_s[h * tq:(h + 1) * tq, :] = qr[:, h * B_ROPE:(h + 1) * B_ROPE].astype(BF16)
    qls = qls_s[...]
    qrs = qrs_s[...]

    def scores(start):
        return (_dot_nt(qls, ckvb_s[pl.ds(start, MLA_TK), :])
                + _dot_nt(qrs, krb_s[pl.ds(start, MLA_TK), :])) * B_SCALE

    jd = (i * tq) // MLA_TK
    dstart = pl.multiple_of(jd * MLA_TK, MLA_TK)
    s = scores(dstart)
    qpos = i * tq + (lax.broadcasted_iota(jnp.int32, s.shape, 0) & (tq - 1))
    kpos = jd * MLA_TK + lax.broadcasted_iota(jnp.int32, s.shape, 1)
    s = jnp.where(qpos >= kpos, s, NEG_INF)
    m = jnp.max(s, axis=1, keepdims=True)
    p = jnp.exp(s - m)
    m_s[...] = m
    l_s[...] = jnp.sum(p, axis=1, keepdims=True)
    acc_s[...] = _dot(p.astype(BF16), ckvb_s[pl.ds(dstart, MLA_TK), :])

    def body(j, carry):
        start = pl.multiple_of(j * MLA_TK, MLA_TK)
        sj = scores(start)
        m_old = m_s[...]
        m_new = jnp.maximum(m_old, jnp.max(sj, axis=1, keepdims=True))
        alpha = jnp.exp(m_old - m_new)
        pj = jnp.exp(sj - m_new)
        l_s[...] = alpha * l_s[...] + jnp.sum(pj, axis=1, keepdims=True)
        acc_s[...] = alpha * acc_s[...] + _dot(pj.astype(BF16), ckvb_s[pl.ds(start, MLA_TK), :])
        m_s[...] = m_new
        return carry

    lax.fori_loop(0, jd, body, 0)

    out = (acc_s[...] / l_s[...]).astype(BF16)
    o_ref[0] = jnp.concatenate([_dot(out[h * tq:(h + 1) * tq], wuv_ref[h]) for h in range(B_HEADS)], axis=1)


def _mla_prompt(q_lat, q_rope, ckv, kr, wuv_t):
    tq = MLA_TQ
    rows = B_HEADS * tq
    return pl.pallas_call(
        _mla_prompt_kernel,
        grid=(BATCH, SEQ // tq),
        in_specs=[pl.BlockSpec((1, tq, B_HEADS * B_KV_LORA), lambda b, i: (b, i, 0)),
                  pl.BlockSpec((1, tq, B_HEADS * B_ROPE), lambda b, i: (b, i, 0)),
                  pl.BlockSpec((1, SEQ, B_KV_LORA), lambda b, i: (b, 0, 0)),
                  pl.BlockSpec((1, SEQ, B_ROPE), lambda b, i: (b, 0, 0)),
                  _resident(wuv_t.shape)],
        out_specs=pl.BlockSpec((1, tq, B_WIDTH), lambda b, i: (b, i, 0)),
        out_shape=jax.ShapeDtypeStruct((BATCH, SEQ, B_WIDTH), F32),
        scratch_shapes=[pltpu.VMEM((SEQ, B_KV_LORA), BF16), pltpu.VMEM((SEQ, B_ROPE), BF16),
                        pltpu.VMEM((rows, B_KV_LORA), BF16), pltpu.VMEM((rows, B_ROPE), BF16),
                        pltpu.VMEM((rows, 1), F32), pltpu.VMEM((rows, 1), F32),
                        pltpu.VMEM((rows, B_KV_LORA), F32)],
        compiler_params=_params(("arbitrary", "arbitrary")),
        name="mla_prompt",
    )(q_lat, q_rope, ckv, kr, wuv_t)


SAMPLE_ROWS = A_HEADS * DEC_SEQ
KV_ROWS = A_GROUP * DEC_SEQ


def _stack_heads(x, n_heads, width):
    return jnp.concatenate([x[:, h * width:(h + 1) * width] for h in range(n_heads)], axis=0)


def _unstack_heads(x, n_heads, tokens):
    return jnp.concatenate([x[h * tokens:(h + 1) * tokens] for h in range(n_heads)], axis=1)


def _causal_new_tokens(shape):
    tq = lax.broadcasted_iota(jnp.int32, shape, 0) & (DEC_SEQ - 1)
    tk = lax.broadcasted_iota(jnp.int32, shape, 1)
    return tk <= tq


def _moba_sample_kernel(pt_ref, q_ref, kn_ref, vn_ref, ka_ref, kb_ref, va_ref, vb_ref, o_ref,
                        qf_s, qb_s, opart_s, m_s, l_s, sg_s):
    del pt_ref
    j = pl.program_id(1)

    @pl.when(j == 0)
    def _():
        qf = _stack_heads(q_ref[0], A_HEADS, A_HEAD_DIM)
        qf_s[...] = qf
        qb_s[...] = qf.astype(BF16)
        m_s[...] = jnp.zeros(m_s.shape, F32)
        l_s[...] = jnp.zeros(l_s.shape, F32)
        sg_s[...] = jnp.zeros(sg_s.shape, F32)

    mcols, lcols, sgcols = [], [], []
    for g in range(A_KV_HEADS):
        rs = slice(g * KV_ROWS, (g + 1) * KV_ROWS)
        kg = jnp.concatenate([ka_ref[0, pl.ds(g, PAGE_SIZE, stride=A_KV_HEADS), :],
                              kb_ref[0, pl.ds(g, PAGE_SIZE, stride=A_KV_HEADS), :]], axis=0)
        vg = jnp.concatenate([va_ref[0, pl.ds(g, PAGE_SIZE, stride=A_KV_HEADS), :],
                              vb_ref[0, pl.ds(g, PAGE_SIZE, stride=A_KV_HEADS), :]], axis=0)
        kmean = jnp.sum(kg, axis=0, keepdims=True) * (1.0 / A_BLOCK)
        sgcols.append(jnp.sum(qf_s[rs, :] * kmean, axis=1, keepdims=True))
        s = _dot_nt(qb_s[rs, :], kg.astype(BF16)) * A_SCALE
        m = jnp.max(s, axis=1, keepdims=True)
        p = jnp.exp(s - m)
        mcols.append(m)
        lcols.append(jnp.sum(p, axis=1, keepdims=True))
        opart_s[j, rs, :] = _dot(p.astype(BF16), vg.astype(BF16))
    lane = lax.broadcasted_iota(jnp.int32, m_s.shape, 1)
    hit = lane == j
    m_s[...] = jnp.where(hit, jnp.concatenate(mcols, axis=0), m_s[...])
    l_s[...] = jnp.where(hit, jnp.concatenate(lcols, axis=0), l_s[...])
    sg_s[...] = jnp.where(hit, jnp.concatenate(sgcols, axis=0), sg_s[...])

    @pl.when(j == N_BLOCKS_PAST - 1)
    def _():
        sg, mm, ll = sg_s[...], m_s[...], l_s[...]
        rank = jnp.zeros(sg.shape, jnp.int32)
        for jp in range(N_BLOCKS_PAST):
            col = sg[:, jp:jp + 1]
            beats = (col > sg) | ((col == sg) & (lane > jp))
            rank = rank + jnp.where(beats, 1, 0)
        sel = (rank < A_TOPK) & (lane < N_BLOCKS_PAST)

        kn, vn = kn_ref[0], vn_ref[0]
        mo, lo, oo = [], [], []
        for g in range(A_KV_HEADS):
            rs = slice(g * KV_ROWS, (g + 1) * KV_ROWS)
            cs = slice(g * A_HEAD_DIM, (g + 1) * A_HEAD_DIM)
            s = _dot_nt(qb_s[rs, :], kn[:, cs].astype(BF16)) * A_SCALE
            s = jnp.where(_causal_new_tokens(s.shape), s, NEG_INF)
            m = jnp.max(s, axis=1, keepdims=True)
            p = jnp.exp(s - m)
            mo.append(m)
            lo.append(jnp.sum(p, axis=1, keepdims=True))
            oo.append(_dot(p.astype(BF16), vn[:, cs].astype(BF16)))
        m_own = jnp.concatenate(mo, axis=0)
        l_own = jnp.concatenate(lo, axis=0)
        o_own = jnp.concatenate(oo, axis=0)

        m_all = jnp.maximum(m_own, jnp.max(jnp.where(sel, mm, NEG_INF), axis=1, keepdims=True))
        w = jnp.where(sel, jnp.exp(mm - m_all), 0.0)
        w_own = jnp.exp(m_own - m_all)
        denom = jnp.sum(w * ll, axis=1, keepdims=True) + w_own * l_own
        acc = w_own * o_own
        for jb in range(N_BLOCKS_PAST):
            acc = acc + w[:, jb:jb + 1] * opart_s[jb]
        o_ref[0] = _unstack_heads(acc / denom, A_HEADS, DEC_SEQ)


def _moba_sample(page_table, qa, ka_new, va_new, cache_k2, cache_v2):
    page_rows = PAGE_SIZE * A_KV_HEADS

    def page_spec(which):
        return pl.BlockSpec((1, page_rows, A_HEAD_DIM),
                            lambda n, j, pt: (pt[n, PAGES_PER_BLOCK * j + which], 0, 0))

    def seq_spec(width):
        return pl.BlockSpec((1, DEC_SEQ, width), lambda n, j, pt: (n, 0, 0))

    grid_spec = pltpu.PrefetchScalarGridSpec(
        num_scalar_prefetch=1,
        grid=(DEC_BATCH, N_BLOCKS_PAST),
        in_specs=[seq_spec(A_WIDTH), seq_spec(A_KV_WIDTH), seq_spec(A_KV_WIDTH),
                  page_spec(0), page_spec(1), page_spec(0), page_spec(1)],
        out_specs=seq_spec(A_WIDTH),
        scratch_shapes=[pltpu.VMEM((SAMPLE_ROWS, A_HEAD_DIM), F32), pltpu.VMEM((SAMPLE_ROWS, A_HEAD_DIM), BF16),
                        pltpu.VMEM((N_BLOCKS_PAST, SAMPLE_ROWS, A_HEAD_DIM), F32),
                        pltpu.VMEM((SAMPLE_ROWS, LANES), F32), pltpu.VMEM((SAMPLE_ROWS, LANES), F32),
                        pltpu.VMEM((SAMPLE_ROWS, LANES), F32)])
    return pl.pallas_call(
        _moba_sample_kernel,
        grid_spec=grid_spec,
        out_shape=jax.ShapeDtypeStruct((DEC_BATCH, DEC_SEQ, A_WIDTH), F32),
        compiler_params=_params(("arbitrary", "arbitrary")),
        name="moba_sample",
    )(page_table, qa, ka_new, va_new, cache_k2, cache_k2, cache_v2, cache_v2)


MLA_PAGES_PER_STEP = 4
MLA_SAMPLE_ROWS = B_HEADS * DEC_SEQ


def _mla_sample_kernel(pt_ref, ql_ref, qr_ref, cn_ref, rn_ref, wuv_ref, *refs):
    del pt_ref
    np_ = MLA_PAGES_PER_STEP
    ckv_refs, krt_refs = refs[:np_], refs[np_:2 * np_]
    o_ref, qls_s, qrs_s, m_s, l_s, acc_s = refs[2 * np_:]
    c = pl.program_id(1)

    @pl.when(c == 0)
    def _():
        qls_s[...] = _stack_heads(ql_ref[0], B_HEADS, B_KV_LORA).astype(BF16)
        qrs_s[...] = _stack_heads(qr_ref[0], B_HEADS, B_ROPE).astype(BF16)
        m_s[...] = jnp.full(m_s.shape, NEG_INF, F32)
        l_s[...] = jnp.zeros(l_s.shape, F32)
        acc_s[...] = jnp.zeros(acc_s.shape, F32)

    qls, qrs = qls_s[...], qrs_s[...]

    def update(s, values):
        m_old = m_s[...]
        m_new = jnp.maximum(m_old, jnp.max(s, axis=1, keepdims=True))
        alpha = jnp.exp(m_old - m_new)
        p = jnp.exp(s - m_new)
        l_s[...] = alpha * l_s[...] + jnp.sum(p, axis=1, keepdims=True)
        acc_s[...] = alpha * acc_s[...] + _dot(p.astype(BF16), values)
        m_s[...] = m_new

    ckv = [r[0].astype(BF16) for r in ckv_refs]
    s = jnp.concatenate([_dot_nt(qls, ckv[u]) + _dot(qrs, krt_refs[u][0].astype(BF16)) for u in range(np_)],
                        axis=1) * B_SCALE
    update(s, jnp.concatenate(ckv, axis=0))

    @pl.when(c == pl.num_programs(1) - 1)
    def _():
        cn = cn_ref[0].astype(BF16)
        sn = (_dot_nt(qls, cn) + _dot_nt(qrs, rn_ref[0].astype(BF16))) * B_SCALE
        sn = jnp.where(_causal_new_tokens(sn.shape), sn, NEG_INF)
        update(sn, cn)
        out = (acc_s[...] / l_s[...]).astype(BF16)
        o_ref[0] = jnp.concatenate(
            [_dot(out[h * DEC_SEQ:(h + 1) * DEC_SEQ], wuv_ref[h]) for h in range(B_HEADS)], axis=1)


def _mla_sample(page_table, q_lat, q_rope, ckv_new, kr_new, wuv_t, cache_ckv2, cache_krt):
    np_ = MLA_PAGES_PER_STEP

    def seq_spec(width):
        return pl.BlockSpec((1, DEC_SEQ, width), lambda n, c, pt: (n, 0, 0))

    def ckv_spec(u):
        return pl.BlockSpec((1, PAGE_SIZE, B_KV_LORA), lambda n, c, pt: (pt[n, np_ * c + u], 0, 0))

    def krt_spec(u):
        return pl.BlockSpec((1, B_ROPE, PAGE_SIZE), lambda n, c, pt: (pt[n, np_ * c + u], 0, 0))

    grid_spec = pltpu.PrefetchScalarGridSpec(
        num_scalar_prefetch=1,
        grid=(DEC_BATCH, N_PAGES // np_),
        in_specs=[seq_spec(B_HEADS * B_KV_LORA), seq_spec(B_HEADS * B_ROPE), seq_spec(B_KV_LORA),
                  seq_spec(B_ROPE),
                  pl.BlockSpec(wuv_t.shape, lambda n, c, pt: (0, 0, 0), pipeline_mode=pl.Buffered(1))]
                 + [ckv_spec(u) for u in range(np_)] + [krt_spec(u) for u in range(np_)],
        out_specs=seq_spec(B_WIDTH),
        scratch_shapes=[pltpu.VMEM((MLA_SAMPLE_ROWS, B_KV_LORA), BF16), pltpu.VMEM((MLA_SAMPLE_ROWS, B_ROPE), BF16),
                        pltpu.VMEM((MLA_SAMPLE_ROWS, 1), F32), pltpu.VMEM((MLA_SAMPLE_ROWS, 1), F32),
                        pltpu.VMEM((MLA_SAMPLE_ROWS, B_KV_LORA), F32)])
    return pl.pallas_call(
        _mla_sample_kernel,
        grid_spec=grid_spec,
        out_shape=jax.ShapeDtypeStruct((DEC_BATCH, DEC_SEQ, B_WIDTH), F32),
        compiler_params=_params(("arbitrary", "arbitrary")),
        name="mla_sample",
    )(page_table, q_lat, q_rope, ckv_new, kr_new, wuv_t, *([cache_ckv2] * np_), *([cache_krt] * np_))


def _out_proj_kernel(oa_ref, ga_ref, ob_ref, gb_ref, x_ref, gate_ref, w_ref, lng_ref, lnb_ref, y_ref,
                     *, per_row_mod):
    u = jnp.concatenate([oa_ref[...] * _silu(ga_ref[...]), ob_ref[...] * _silu(gb_ref[...])], axis=1)
    o = _dot(u.astype(BF16), w_ref[...])
    gate = gate_ref[...] if per_row_mod else gate_ref[0]
    r = ALPHA * x_ref[...] + gate * o
    mu = jnp.mean(r, axis=-1, keepdims=True)
    d = r - mu
    var = jnp.mean(d * d, axis=-1, keepdims=True)
    y_ref[...] = d * lax.rsqrt(var + LN_EPS) * lng_ref[...] + lnb_ref[...]


def _out_proj(oa, ga, ob, gb, x, mod, w_out, ln_g, ln_b, *, per_row_mod, rows_per_batch):
    rows = x.shape[0]
    tm = 256
    if per_row_mod:
        gate_spec = pl.BlockSpec((tm, D_MODEL), lambda i: (i, 2))
    else:
        tiles_per_batch = rows_per_batch // tm
        gate_spec = pl.BlockSpec((1, 1, D_MODEL), lambda i: (i // tiles_per_batch, 0, 2))
    half = pl.BlockSpec((tm, A_WIDTH), lambda i: (i, 0))
    full = pl.BlockSpec((tm, D_MODEL), lambda i: (i, 0))
    return pl.pallas_call(
        functools.partial(_out_proj_kernel, per_row_mod=per_row_mod),
        grid=(rows // tm,),
        in_specs=[half, half, half, half, full, gate_spec,
                  _resident(w_out.shape), _resident(ln_g.shape), _resident(ln_b.shape)],
        out_specs=full,
        out_shape=jax.ShapeDtypeStruct((rows, D_MODEL), F32),
        compiler_params=_params(("arbitrary",)),
        name="out_proj_sample" if per_row_mod else "out_proj_prompt",
    )(oa, ga, ob, gb, x, mod, w_out, ln_g, ln_b)


def _rope_tables(pos, half):
    inv = jnp.power(ROPE_THETA, -jnp.arange(half, dtype=F32) / half)
    ang = pos.astype(F32)[:, None] * inv[None, :]
    cos, sin = jnp.cos(ang), jnp.sin(ang)
    return jnp.concatenate([cos, cos], axis=1), jnp.concatenate([-sin, sin], axis=1)


def _rope_tables_128(pos):
    cosa, sina = _rope_tables(pos, A_HEAD_DIM // 2)
    cosb, sinb = _rope_tables(pos, B_ROPE // 2)
    return cosa, sina, jnp.tile(cosb, (1, 2)), jnp.tile(sinb, (1, 2))


def kernel(x_prompt, x_sample, cache_k, cache_v, cache_ckv, cache_kr, page_table, c_prompt, c_sample,
           w_ada, b_ada, w_in, q_norm_g, w_uq, kv_norm_g, w_uk, w_uv, w_out, ln_g, ln_b):
    layer = 0
    n_pool = cache_k.shape[1]
    rows_p = BATCH * SEQ
    rows_s = DEC_BATCH * DEC_SEQ

    w = w_in[layer]
    w_p = jnp.concatenate([w[:, :3840], w[:, 3904:], w[:, 3840:3904], jnp.zeros((D_MODEL, 64), F32)],
                          axis=1).astype(BF16)
    wuq = w_uq[layer].reshape(B_Q_LORA, B_HEADS, B_NOPE + B_ROPE)
    wuq_n = wuq[:, :, :B_NOPE].reshape(B_Q_LORA, B_HEADS * B_NOPE).astype(BF16)
    wuq_r = wuq[:, :, B_NOPE:].reshape(B_Q_LORA, B_HEADS * B_ROPE).astype(BF16)
    wuk_t = jnp.transpose(w_uk[layer], (1, 2, 0)).astype(BF16)
    wuv_t = jnp.transpose(w_uv[layer], (1, 0, 2)).astype(BF16)
    w_o = w_out[layer].astype(BF16)
    qg, kvg, lng, lnb = q_norm_g[layer][None], kv_norm_g[layer][None], ln_g[layer][None], ln_b[layer][None]

    n_c = BATCH + DEC_BATCH
    pad = (-n_c) % 8
    c_all = jnp.concatenate([c_prompt, c_sample, jnp.zeros((pad, D_MODEL), F32)], axis=0)
    mod = _ada_mod(c_all, w_ada[layer], b_ada[layer][None])
    mod_p = mod[:BATCH].reshape(BATCH, 1, 3 * D_MODEL)
    mod_s = jnp.repeat(mod[BATCH:n_c], DEC_SEQ, axis=0)

    tabs_p = _rope_tables_128(jnp.arange(SEQ, dtype=jnp.int32))
    tm = 256
    tabs_s = tuple(jnp.tile(t, (tm // DEC_SEQ, 1))
                   for t in _rope_tables_128(PAST_LEN + jnp.arange(DEC_SEQ, dtype=jnp.int32)))

    xp = x_prompt.reshape(rows_p, D_MODEL)
    xs = x_sample.reshape(rows_s, D_MODEL)
    proj_w = (w_p, wuq_n, wuq_r, wuk_t, qg, kvg)
    qa_p, ka_p, va_p, ga_p, ql_p, qr_p, ckv_p, kr_p, gb_p = _in_proj(
        xp, mod_p, *proj_w, tabs_p, per_row_mod=False, rows_per_batch=SEQ)
    qa_s, ka_s, va_s, ga_s, ql_s, qr_s, ckv_s, kr_s, gb_s = _in_proj(
        xs, mod_s, *proj_w, tabs_s, per_row_mod=True, rows_per_batch=DEC_SEQ)

    oa_p = _moba_prompt(qa_p.reshape(BATCH, SEQ, A_WIDTH), ka_p.reshape(BATCH, SEQ, A_KV_WIDTH),
                        va_p.reshape(BATCH, SEQ, A_KV_WIDTH))
    ob_p = _mla_prompt(ql_p.reshape(BATCH, SEQ, -1), qr_p.reshape(BATCH, SEQ, -1),
                       ckv_p.reshape(BATCH, SEQ, B_KV_LORA), kr_p.reshape(BATCH, SEQ, B_ROPE), wuv_t)
    y_p = _out_proj(oa_p.reshape(rows_p, A_WIDTH), ga_p, ob_p.reshape(rows_p, B_WIDTH), gb_p, xp, mod_p,
                    w_o, lng, lnb, per_row_mod=False, rows_per_batch=SEQ)

    cache_k2 = cache_k[layer].reshape(n_pool, PAGE_SIZE * A_KV_HEADS, A_HEAD_DIM)
    cache_v2 = cache_v[layer].reshape(n_pool, PAGE_SIZE * A_KV_HEADS, A_HEAD_DIM)
    cache_krt = jnp.swapaxes(cache_kr[layer], 1, 2)
    oa_s = _moba_sample(page_table, qa_s.reshape(DEC_BATCH, DEC_SEQ, A_WIDTH),
                        ka_s.reshape(DEC_BATCH, DEC_SEQ, A_KV_WIDTH), va_s.reshape(DEC_BATCH, DEC_SEQ, A_KV_WIDTH),
                        cache_k2, cache_v2)
    ob_s = _mla_sample(page_table, ql_s.reshape(DEC_BATCH, DEC_SEQ, -1), qr_s.reshape(DEC_BATCH, DEC_SEQ, -1),
                       ckv_s.reshape(DEC_BATCH, DEC_SEQ, B_KV_LORA), kr_s.reshape(DEC_BATCH, DEC_SEQ, B_ROPE),
                       wuv_t, cache_ckv[layer], cache_krt)
    y_s = _out_proj(oa_s.reshape(rows_s, A_WIDTH), ga_s, ob_s.reshape(rows_s, B_WIDTH), gb_s, xs, mod_s,
                    w_o, lng, lnb, per_row_mod=True, rows_per_batch=DEC_SEQ)

    return (y_p.reshape(BATCH, SEQ, D_MODEL), y_s.reshape(DEC_BATCH, DEC_SEQ, D_MODEL),
            ka_p.reshape(DEPTH, BATCH, SEQ, A_KV_HEADS, A_HEAD_DIM),
            va_p.reshape(DEPTH, BATCH, SEQ, A_KV_HEADS, A_HEAD_DIM),
            ckv_p.reshape(DEPTH, BATCH, SEQ, B_KV_LORA), kr_p.reshape(DEPTH, BATCH, SEQ, B_ROPE),
            ka_s.reshape(DEPTH, DEC_BATCH, DEC_SEQ, A_KV_HEADS, A_HEAD_DIM),
            va_s.reshape(DEPTH, DEC_BATCH, DEC_SEQ, A_KV_HEADS, A_HEAD_DIM),
            ckv_s.reshape(DEPTH, DEC_BATCH, DEC_SEQ, B_KV_LORA), kr_s.reshape(DEPTH, DEC_BATCH, DEC_SEQ, B_ROPE))
```

```python
import functools

import jax
import jax.numpy as jnp
from jax import lax
from jax.experimental import pallas as pl
from jax.experimental.pallas import tpu as pltpu

F32 = jnp.float32
BF16 = jnp.bfloat16
NEG_INF = float("-inf")

D_MODEL = 2048
BATCH = 4
SEQ = 2048
DEC_BATCH = 128
DEC_SEQ = 8
PAST_LEN = 8192
PAGE_SIZE = 128
N_PAGES = PAST_LEN // PAGE_SIZE

A_HEAD_DIM = 128
A_HEADS = 8
A_KV_HEADS = 4
A_GROUP = A_HEADS // A_KV_HEADS
A_WIDTH = A_HEADS * A_HEAD_DIM
A_KV_WIDTH = A_KV_HEADS * A_HEAD_DIM
A_BLOCK = 256
A_TOPK = 3
A_SCALE = A_HEAD_DIM ** -0.5
B_NOPE = 128
B_ROPE = 64
B_VDIM = 128
B_HEADS = 8
B_WIDTH = B_HEADS * B_VDIM
B_Q_LORA = 512
B_KV_LORA = 256
B_SCALE = (B_NOPE + B_ROPE) ** -0.5
ROPE_THETA = 10000.0
RMS_EPS = 1e-6
LN_EPS = 1e-5
DEPTH = 1
ALPHA = (2 * DEPTH) ** 0.25

N_BLOCKS_PROMPT = SEQ // A_BLOCK
N_BLOCKS_PAST = PAST_LEN // A_BLOCK
PAGES_PER_BLOCK = A_BLOCK // PAGE_SIZE

C_QA = (0, 1024)
C_KA = (1024, 1536)
C_VA = (1536, 2048)
C_GA = (2048, 3072)
C_CQ = (3072, 3584)
C_CKV = (3584, 3840)
C_GB = (3840, 4864)
C_KR = (4864, 4992)
W_IN_COLS = 4992

LANES = 128
VMEM_LIMIT_BYTES = 56 * 1024 * 1024


def _params(semantics):
    return pltpu.CompilerParams(dimension_semantics=semantics, vmem_limit_bytes=VMEM_LIMIT_BYTES)


def _silu(x):
    return x * jax.nn.sigmoid(x)


def _dot(a, b):
    return jnp.dot(a, b, preferred_element_type=F32)


def _dot_nt(a, b):
    return lax.dot_general(a, b, (((1,), (1,)), ((), ())), preferred_element_type=F32)


def _resident(shape):
    nd = len(shape)
    return pl.BlockSpec(shape, lambda *_: (0,) * nd, pipeline_mode=pl.Buffered(1))


def _ada_kernel(c_ref, w_ref, b_ref, o_ref):
    a = _silu(c_ref[...]).astype(BF16)
    o_ref[...] = _dot(a, w_ref[...].astype(BF16)) + b_ref[...]


def _ada_mod(c_all, w_ada, b_ada):
    rows = c_all.shape[0]
    n = w_ada.shape[1]
    tn = 768
    return pl.pallas_call(
        _ada_kernel,
        grid=(n // tn,),
        in_specs=[pl.BlockSpec((rows, D_MODEL), lambda j: (0, 0)),
                  pl.BlockSpec((D_MODEL, tn), lambda j: (0, j)),
                  pl.BlockSpec((1, tn), lambda j: (0, j))],
        out_specs=pl.BlockSpec((rows, tn), lambda j: (0, j)),
        out_shape=jax.ShapeDtypeStruct((rows, n), F32),
        compiler_params=_params(("arbitrary",)),
        name="ada_mod",
    )(c_all, w_ada, b_ada)


def _in_proj_kernel(x_ref, shift_ref, scale_ref, w_ref, wuqn_ref, wuqr_ref, wuk_ref, qg_ref, kvg_ref,
                    cosa_ref, sina_ref, cosb_ref, sinb_ref,
                    qa_ref, ka_ref, va_ref, ga_ref, qlat_ref, qrope_ref, ckv_ref, kr_ref, gb_ref,
                    *, per_row_mod):
    x = x_ref[...]
    if per_row_mod:
        shift, scale = shift_ref[...], scale_ref[...]
    else:
        shift, scale = shift_ref[0], scale_ref[0]
    hb = (x * (1.0 + scale) + shift).astype(BF16)
    tm = x.shape[0]
    cosa, sina = cosa_ref[...], sina_ref[...]
    cosb, sinb = cosb_ref[...], sinb_ref[...]

    def proj(cols):
        return _dot(hb, w_ref[:, cols[0]:cols[1]])

    def rope_full(z, n_heads):
        outs = []
        for h in range(n_heads):
            zh = z[:, h * LANES:(h + 1) * LANES]
            outs.append(zh * cosa + pltpu.roll(zh, 64, 1) * sina)
        return jnp.concatenate(outs, axis=1)

    lane = lax.broadcasted_iota(jnp.int32, (tm, LANES), 1)
    first_half = (lane & 63) < 32

    def rope_half(zg):
        partner = jnp.where(first_half, pltpu.roll(zg, 96, 1), pltpu.roll(zg, 32, 1))
        return zg * cosb + partner * sinb

    def rms(z, g):
        return z * lax.rsqrt(jnp.mean(z * z, axis=-1, keepdims=True) + RMS_EPS) * g

    qa_ref[...] = rope_full(proj(C_QA), A_HEADS)
    ka_ref[...] = rope_full(proj(C_KA), A_KV_HEADS)
    va_ref[...] = proj(C_VA)
    ga_ref[...] = proj(C_GA)
    gb_ref[...] = proj(C_GB)
    ckv_ref[...] = rms(proj(C_CKV), kvg_ref[...])
    kr_ref[...] = rope_half(proj(C_KR))[:, :B_ROPE]

    cqn = rms(proj(C_CQ), qg_ref[...]).astype(BF16)
    qn = _dot(cqn, wuqn_ref[...])
    qr = _dot(cqn, wuqr_ref[...])
    for h in range(B_HEADS):
        qlat_ref[:, h * B_KV_LORA:(h + 1) * B_KV_LORA] = _dot(
            qn[:, h * B_NOPE:(h + 1) * B_NOPE].astype(BF16), wuk_ref[h])
    qrope_ref[...] = jnp.concatenate(
        [rope_half(qr[:, g * LANES:(g + 1) * LANES]) for g in range(B_HEADS * B_ROPE // LANES)], axis=1)


def _in_proj(x, mod, w_p, wuq_n, wuq_r, wuk_t, q_norm_g, kv_norm_g, tabs, *, per_row_mod, rows_per_batch):
    rows = x.shape[0]
    tm = 256
    steps = rows // tm
    if per_row_mod:
        shift_spec = pl.BlockSpec((tm, D_MODEL), lambda i: (i, 0))
        scale_spec = pl.BlockSpec((tm, D_MODEL), lambda i: (i, 1))
        tab_spec = pl.BlockSpec((tm, LANES), lambda i: (0, 0))
    else:
        tiles_per_batch = rows_per_batch // tm
        shift_spec = pl.BlockSpec((1, 1, D_MODEL), lambda i: (i // tiles_per_batch, 0, 0))
        scale_spec = pl.BlockSpec((1, 1, D_MODEL), lambda i: (i // tiles_per_batch, 0, 1))
        tab_spec = pl.BlockSpec((tm, LANES), lambda i: (i % tiles_per_batch, 0))
    widths = (A_WIDTH, A_KV_WIDTH, A_KV_WIDTH, A_WIDTH, B_HEADS * B_KV_LORA, B_HEADS * B_ROPE,
              B_KV_LORA, B_ROPE, B_WIDTH)
    return pl.pallas_call(
        functools.partial(_in_proj_kernel, per_row_mod=per_row_mod),
        grid=(steps,),
        in_specs=[pl.BlockSpec((tm, D_MODEL), lambda i: (i, 0)), shift_spec, scale_spec,
                  _resident(w_p.shape), _resident(wuq_n.shape), _resident(wuq_r.shape),
                  _resident(wuk_t.shape), _resident(q_norm_g.shape), _resident(kv_norm_g.shape),
                  tab_spec, tab_spec, tab_spec, tab_spec],
        out_specs=[pl.BlockSpec((tm, w), lambda i: (i, 0)) for w in widths],
        out_shape=[jax.ShapeDtypeStruct((rows, w), F32) for w in widths],
        compiler_params=_params(("arbitrary",)),
        name="in_proj_sample" if per_row_mod else "in_proj_prompt",
    )(x, mod, mod, w_p, wuq_n, wuq_r, wuk_t, q_norm_g, kv_norm_g, *tabs)


MOBA_ROW_CHUNK = 128


def _moba_prompt_kernel(q_ref, k_ref, v_ref, o_ref, kb_s, vb_s, km_s, qb_s, m_s, l_s, acc_s):
    i = pl.program_id(2)
    rows = A_GROUP * A_BLOCK
    rc = MOBA_ROW_CHUNK
    chunks = [slice(c * rc, (c + 1) * rc) for c in range(rows // rc)]

    @pl.when(i == 0)
    def _():
        k = k_ref[0]
        kb_s[...] = k.astype(BF16)
        vb_s[...] = v_ref[0].astype(BF16)
        for j in range(N_BLOCKS_PROMPT):
            km_s[j:j + 1, :] = jnp.sum(k[j * A_BLOCK:(j + 1) * A_BLOCK], axis=0, keepdims=True) * (1.0 / A_BLOCK)

    q2 = q_ref[0]
    qb = jnp.concatenate([q2[:, g * A_HEAD_DIM:(g + 1) * A_HEAD_DIM] for g in range(A_GROUP)],
                         axis=0).astype(BF16)
    qb_s[...] = qb

    sg = _dot_nt(qb, km_s[...].astype(BF16))
    blk = lax.broadcasted_iota(jnp.int32, sg.shape, 1)
    rank = jnp.zeros(sg.shape, jnp.int32)
    for jp in range(N_BLOCKS_PROMPT - 1):
        col = sg[:, jp:jp + 1]
        beats = (col > sg) | ((col == sg) & (blk > jp))
        rank = rank + jnp.where(beats, jnp.where(jp < i, 1, 0), 0)
    sel = (blk < i) & (rank < A_TOPK)

    start = pl.multiple_of(i * A_BLOCK, A_BLOCK)
    k_own = kb_s[pl.ds(start, A_BLOCK), :]
    v_own = vb_s[pl.ds(start, A_BLOCK), :]
    s_own = [_dot_nt(qb_s[rs, :], k_own) for rs in chunks]
    p_own = []
    for c, (rs, s) in enumerate(zip(chunks, s_own)):
        tok = (c * rc + lax.broadcasted_iota(jnp.int32, s.shape, 0)) & (A_BLOCK - 1)
        key = lax.broadcasted_iota(jnp.int32, s.shape, 1)
        s = jnp.where(tok >= key, s * A_SCALE, NEG_INF)
        m = jnp.max(s, axis=1, keepdims=True)
        p = jnp.exp(s - m)
        m_s[rs, :] = m
        l_s[rs, :] = jnp.sum(p, axis=1, keepdims=True)
        p_own.append(p.astype(BF16))
    for rs, p in zip(chunks, p_own):
        acc_s[rs, :] = _dot(p, v_own)

    for j in range(N_BLOCKS_PROMPT - 1):
        @pl.when(j < i)
        def _(j=j):
            k_j = kb_s[j * A_BLOCK:(j + 1) * A_BLOCK, :]
            v_j = vb_s[j * A_BLOCK:(j + 1) * A_BLOCK, :]
            sjs = [_dot_nt(qb_s[rs, :], k_j) for rs in chunks]
            pjs, alphas = [], []
            for rs, sj in zip(chunks, sjs):
                sj = jnp.where(sel[rs, j:j + 1], sj * A_SCALE, NEG_INF)
                m_old = m_s[rs, :]
                m_new = jnp.maximum(m_old, jnp.max(sj, axis=1, keepdims=True))
                alpha = jnp.exp(m_old - m_new)
                pj = jnp.exp(sj - m_new)
                l_s[rs, :] = alpha * l_s[rs, :] + jnp.sum(pj, axis=1, keepdims=True)
                m_s[rs, :] = m_new
                pjs.append(pj.astype(BF16))
                alphas.append(alpha)
            for rs, pj, alpha in zip(chunks, pjs, alphas):
                acc_s[rs, :] = alpha * acc_s[rs, :] + _dot(pj, v_j)

    out = acc_s[...] / l_s[...]
    o_ref[0] = jnp.concatenate([out[g * A_BLOCK:(g + 1) * A_BLOCK] for g in range(A_GROUP)], axis=1)


def _moba_prompt(qa, ka, va):
    rows = A_GROUP * A_BLOCK
    gw = A_GROUP * A_HEAD_DIM
    return pl.pallas_call(
        _moba_prompt_kernel,
        grid=(BATCH, A_KV_HEADS, N_BLOCKS_PROMPT),
        in_specs=[pl.BlockSpec((1, A_BLOCK, gw), lambda b, g, i: (b, i, g)),
                  pl.BlockSpec((1, SEQ, A_HEAD_DIM), lambda b, g, i: (b, 0, g)),
                  pl.BlockSpec((1, SEQ, A_HEAD_DIM), lambda b, g, i: (b, 0, g))],
        out_specs=pl.BlockSpec((1, A_BLOCK, gw), lambda b, g, i: (b, i, g)),
        out_shape=jax.ShapeDtypeStruct((BATCH, SEQ, A_WIDTH), F32),
        scratch_shapes=[pltpu.VMEM((SEQ, A_HEAD_DIM), BF16), pltpu.VMEM((SEQ, A_HEAD_DIM), BF16),
                        pltpu.VMEM((N_BLOCKS_PROMPT, A_HEAD_DIM), F32),
                        pltpu.VMEM((rows, A_HEAD_DIM), BF16),
                        pltpu.VMEM((rows, 1), F32), pltpu.VMEM((rows, 1), F32),
                        pltpu.VMEM((rows, A_HEAD_DIM), F32)],
        compiler_params=_params(("arbitrary", "arbitrary", "arbitrary")),
        name="moba_prompt",
    )(qa, ka, va)


MLA_TQ = 128
MLA_TK = 256
MLA_ROW_CHUNK = 256


def _mla_prompt_kernel(ql_ref, qr_ref, ckv_ref, kr_ref, wuv_ref, o_ref,
                       ckvb_s, krb_s, qls_s, qrs_s, m_s, l_s, acc_s):
    i = pl.program_id(1)
    tq = MLA_TQ
    rc = MLA_ROW_CHUNK
    chunks = [slice(c * rc, (c + 1) * rc) for c in range(B_HEADS * tq // rc)]

    @pl.when(i == 0)
    def _():
        ckvb_s[...] = ckv_ref[0].astype(BF16)
        krb_s[...] = kr_ref[0].astype(BF16)

    ql = ql_ref[0]
    qr = qr_ref[0]
    for h in range(B_HEADS):
        qls_s[h * tq:(h + 1) * tq, :] = ql[:, h * B_KV_LORA:(h + 1) * B_KV_LORA].astype(BF16)
        qrs_s[h * tq:(h + 1) * tq, :] = qr[:, h * B_ROPE:(h + 1) * B_ROPE].astype(BF16)

    def scores(rs, ckv_blk, kr_blk):
        return (_dot_nt(qls_s[rs, :], ckv_blk) + _dot_nt(qrs_s[rs, :], kr_blk)) * B_SCALE

    jd = (i * tq) // MLA_TK
    dstart = pl.multiple_of(jd * MLA_TK, MLA_TK)
    ckv_d = ckvb_s[pl.ds(dstart, MLA_TK), :]
    kr_d = krb_s[pl.ds(dstart, MLA_TK), :]
    sds = [scores(rs, ckv_d, kr_d) for rs in chunks]
    pds = []
    for c, (rs, s) in enumerate(zip(chunks, sds)):
        qpos = i * tq + ((c * rc + lax.broadcasted_iota(jnp.int32, s.shape, 0)) & (tq - 1))
        kpos = jd * MLA_TK + lax.broadcasted_iota(jnp.int32, s.shape, 1)
        s = jnp.where(qpos >= kpos, s, NEG_INF)
        m = jnp.max(s, axis=1, keepdims=True)
        p = jnp.exp(s - m)
        m_s[rs, :] = m
        l_s[rs, :] = jnp.sum(p, axis=1, keepdims=True)
        pds.append(p.astype(BF16))
    for rs, p in zip(chunks, pds):
        acc_s[rs, :] = _dot(p, ckv_d)

    def body(j, carry):
        start = pl.multiple_of(j * MLA_TK, MLA_TK)
        ckv_j = ckvb_s[pl.ds(start, MLA_TK), :]
        kr_j = krb_s[pl.ds(start, MLA_TK), :]
        sjs = [scores(rs, ckv_j, kr_j) for rs in chunks]
        pjs, alphas = [], []
        for rs, sj in zip(chunks, sjs):
            m_old = m_s[rs, :]
            m_new = jnp.maximum(m_old, jnp.max(sj, axis=1, keepdims=True))
            alpha = jnp.exp(m_old - m_new)
            pj = jnp.exp(sj - m_new)
            l_s[rs, :] = alpha * l_s[rs, :] + jnp.sum(pj, axis=1, keepdims=True)
            m_s[rs, :] = m_new
            pjs.append(pj.astype(BF16))
            alphas.append(alpha)
        for rs, pj, alpha in zip(chunks, pjs, alphas):
            acc_s[rs, :] = alpha * acc_s[rs, :] + _dot(pj, ckv_j)
        return carry

    lax.fori_loop(0, jd, body, 0)

    out = (acc_s[...] / l_s[...]).astype(BF16)
    o_ref[0] = jnp.concatenate([_dot(out[h * tq:(h + 1) * tq], wuv_ref[h]) for h in range(B_HEADS)], axis=1)


def _mla_prompt(q_lat, q_rope, ckv, kr, wuv_t):
    tq = MLA_TQ
    rows = B_HEADS * tq
    return pl.pallas_call(
        _mla_prompt_kernel,
        grid=(BATCH, SEQ // tq),
        in_specs=[pl.BlockSpec((1, tq, B_HEADS * B_KV_LORA), lambda b, i: (b, i, 0)),
                  pl.BlockSpec((1, tq, B_HEADS * B_ROPE), lambda b, i: (b, i, 0)),
                  pl.BlockSpec((1, SEQ, B_KV_LORA), lambda b, i: (b, 0, 0)),
                  pl.BlockSpec((1, SEQ, B_ROPE), lambda b, i: (b, 0, 0)),
                  _resident(wuv_t.shape)],
        out_specs=pl.BlockSpec((1, tq, B_WIDTH), lambda b, i: (b, i, 0)),
        out_shape=jax.ShapeDtypeStruct((BATCH, SEQ, B_WIDTH), F32),
        scratch_shapes=[pltpu.VMEM((SEQ, B_KV_LORA), BF16), pltpu.VMEM((SEQ, B_ROPE), BF16),
                        pltpu.VMEM((rows, B_KV_LORA), BF16), pltpu.VMEM((rows, B_ROPE), BF16),
                        pltpu.VMEM((rows, 1), F32), pltpu.VMEM((rows, 1), F32),
                        pltpu.VMEM((rows, B_KV_LORA), F32)],
        compiler_params=_params(("arbitrary", "arbitrary")),
        name="mla_prompt",
    )(q_lat, q_rope, ckv, kr, wuv_t)


SAMPLE_ROWS = A_HEADS * DEC_SEQ
KV_ROWS = A_GROUP * DEC_SEQ


def _stack_heads(x, n_heads, width):
    return jnp.concatenate([x[:, h * width:(h + 1) * width] for h in range(n_heads)], axis=0)


def _unstack_heads(x, n_heads, tokens):
    return jnp.concatenate([x[h * tokens:(h + 1) * tokens] for h in range(n_heads)], axis=1)


def _causal_new_tokens(shape):
    tq = lax.broadcasted_iota(jnp.int32, shape, 0) & (DEC_SEQ - 1)
    tk = lax.broadcasted_iota(jnp.int32, shape, 1)
    return tk <= tq


def _log2(n):
    assert n & (n - 1) == 0
    return n.bit_length() - 1


def _set_lane(acc, lane, idx, col):
    return jnp.where(lane == idx, col, acc)


MOBA_GROUP = 4
MOBA_GROUPS = N_BLOCKS_PAST // MOBA_GROUP
PAGE_ROWS = PAGE_SIZE * A_KV_HEADS
BLOCK_ROWS = A_BLOCK * A_KV_HEADS
N_HALVES = 2


def _moba_sample_kernel(pt_ref, q_ref, kn_ref, vn_ref, ck_ref, cv_ref, o_ref, kbuf, vbuf, sems, opart_s):
    n = pl.program_id(0)
    total = pl.num_programs(0) * MOBA_GROUPS

    def group_copies(seq, grp, half):
        copies = []
        for u in range(MOBA_GROUP * PAGES_PER_BLOCK):
            page = pt_ref[seq, MOBA_GROUP * PAGES_PER_BLOCK * grp + u]
            rows = pl.ds(u * PAGE_ROWS, PAGE_ROWS)
            copies.append(pltpu.make_async_copy(ck_ref.at[page], kbuf.at[half, rows], sems.at[0, half]))
            copies.append(pltpu.make_async_copy(cv_ref.at[page], vbuf.at[half, rows], sems.at[1, half]))
        return copies

    @pl.when(n == 0)
    def _():
        for cp in group_copies(0, 0, 0):
            cp.start()

    qf = _stack_heads(q_ref[0], A_HEADS, A_HEAD_DIM)
    qb = qf.astype(BF16)
    row_head = lax.shift_right_logical(lax.broadcasted_iota(jnp.int32, (SAMPLE_ROWS, BLOCK_ROWS), 0),
                                       _log2(KV_ROWS))
    col_head = lax.broadcasted_iota(jnp.int32, (SAMPLE_ROWS, BLOCK_ROWS), 1) & (A_KV_HEADS - 1)
    same_head = row_head == col_head
    lane = lax.broadcasted_iota(jnp.int32, (SAMPLE_ROWS, LANES), 1)

    def group_partials(j0, half, stats):
        m_all, l_all, sg_all = stats
        block_rows = [slice(b * BLOCK_ROWS, (b + 1) * BLOCK_ROWS) for b in range(MOBA_GROUP)]
        scores, gates = [], []
        for rows in block_rows:
            kf = kbuf[half, rows, :]
            ks = jnp.sum(kf.reshape(BLOCK_ROWS // 8, 8, A_HEAD_DIM), axis=0)
            km = (ks[:A_KV_HEADS] + ks[A_KV_HEADS:]) * (1.0 / A_BLOCK)
            km_rows = jnp.concatenate(
                [jnp.broadcast_to(km[g:g + 1], (KV_ROWS, A_HEAD_DIM)) for g in range(A_KV_HEADS)], axis=0)
            gates.append(jnp.sum(qf * km_rows, axis=1, keepdims=True))
            scores.append(_dot_nt(qb, kf.astype(BF16)))
        probs = []
        for b in range(MOBA_GROUP):
            s = jnp.where(same_head, scores[b] * A_SCALE, NEG_INF)
            m = jnp.max(s, axis=1, keepdims=True)
            p = jnp.exp(s - m)
            probs.append(p.astype(BF16))
            m_all = _set_lane(m_all, lane, j0 + b, m)
            l_all = _set_lane(l_all, lane, j0 + b, jnp.sum(p, axis=1, keepdims=True))
            sg_all = _set_lane(sg_all, lane, j0 + b, gates[b])
        for b, rows in enumerate(block_rows):
            opart_s[j0 + b] = _dot(probs[b], vbuf[half, rows, :].astype(BF16))
        return m_all, l_all, sg_all

    def ring_step(jj, stats):
        for half in range(N_HALVES):
            grp = jj * N_HALVES + half
            nxt = n * MOBA_GROUPS + grp + 1

            @pl.when(nxt < total)
            def _():
                for cp in group_copies(lax.shift_right_logical(nxt, _log2(MOBA_GROUPS)),
                                       nxt & (MOBA_GROUPS - 1), (half + 1) % N_HALVES):
                    cp.start()

            for cp in group_copies(n, grp, half):
                cp.wait()
            stats = group_partials(grp * MOBA_GROUP, half, stats)
        return stats

    zeros = jnp.zeros((SAMPLE_ROWS, LANES), F32)
    mm, ll, sg = lax.fori_loop(0, MOBA_GROUPS // N_HALVES, ring_step, (zeros, zeros, zeros))

    rank = jnp.zeros(sg.shape, jnp.int32)
    for jp in range(N_BLOCKS_PAST):
        col = sg[:, jp:jp + 1]
        beats = (col > sg) | ((col == sg) & (lane > jp))
        rank = rank + jnp.where(beats, 1, 0)
    sel = (rank < A_TOPK) & (lane < N_BLOCKS_PAST)

    kn, vn = kn_ref[0], vn_ref[0]
    mo, lo, oo = [], [], []
    for g in range(A_KV_HEADS):
        rs = slice(g * KV_ROWS, (g + 1) * KV_ROWS)
        cs = slice(g * A_HEAD_DIM, (g + 1) * A_HEAD_DIM)
        s = _dot_nt(qb[rs], kn[:, cs].astype(BF16)) * A_SCALE
        s = jnp.where(_causal_new_tokens(s.shape), s, NEG_INF)
        m = jnp.max(s, axis=1, keepdims=True)
        p = jnp.exp(s - m)
        mo.append(m)
        lo.append(jnp.sum(p, axis=1, keepdims=True))
        oo.append(_dot(p.astype(BF16), vn[:, cs].astype(BF16)))
    m_own = jnp.concatenate(mo, axis=0)
    l_own = jnp.concatenate(lo, axis=0)
    o_own = jnp.concatenate(oo, axis=0)

    m_all = jnp.maximum(m_own, jnp.max(jnp.where(sel, mm, NEG_INF), axis=1, keepdims=True))
    w = jnp.where(sel, jnp.exp(mm - m_all), 0.0)
    w_own = jnp.exp(m_own - m_all)
    denom = jnp.sum(w * ll, axis=1, keepdims=True) + w_own * l_own
    acc = w_own * o_own
    for jb in range(N_BLOCKS_PAST):
        acc = acc + w[:, jb:jb + 1] * opart_s[jb]
    o_ref[0] = _unstack_heads(acc / denom, A_HEADS, DEC_SEQ)


def _moba_sample(page_table, qa, ka_new, va_new, cache_k2, cache_v2):
    assert MOBA_GROUPS % N_HALVES == 0

    def seq_spec(width):
        return pl.BlockSpec((1, DEC_SEQ, width), lambda n, pt: (n, 0, 0))

    grid_spec = pltpu.PrefetchScalarGridSpec(
        num_scalar_prefetch=1,
        grid=(DEC_BATCH,),
        in_specs=[seq_spec(A_WIDTH), seq_spec(A_KV_WIDTH), seq_spec(A_KV_WIDTH),
                  pl.BlockSpec(memory_space=pl.ANY), pl.BlockSpec(memory_space=pl.ANY)],
        out_specs=seq_spec(A_WIDTH),
        scratch_shapes=[pltpu.VMEM((N_HALVES, MOBA_GROUP * BLOCK_ROWS, A_HEAD_DIM), F32),
                        pltpu.VMEM((N_HALVES, MOBA_GROUP * BLOCK_ROWS, A_HEAD_DIM), F32),
                        pltpu.SemaphoreType.DMA((2, N_HALVES)),
                        pltpu.VMEM((N_BLOCKS_PAST, SAMPLE_ROWS, A_HEAD_DIM), F32)])
    return pl.pallas_call(
        _moba_sample_kernel,
        grid_spec=grid_spec,
        out_shape=jax.ShapeDtypeStruct((DEC_BATCH, DEC_SEQ, A_WIDTH), F32),
        compiler_params=_params(("arbitrary",)),
        name="moba_sample",
    )(page_table, qa, ka_new, va_new, cache_k2, cache_v2)


MLA_GROUP_PAGES = 16
MLA_GROUPS = N_PAGES // MLA_GROUP_PAGES
MLA_CHUNK_PAGES = 4
MLA_CHUNK_KEYS = MLA_CHUNK_PAGES * PAGE_SIZE
MLA_GROUP_CHUNKS = MLA_GROUP_PAGES // MLA_CHUNK_PAGES
MLA_CHUNKS = N_PAGES // MLA_CHUNK_PAGES
MLA_SAMPLE_ROWS = B_HEADS * DEC_SEQ


def _mla_sample_kernel(pt_ref, ql_ref, qr_ref, cn_ref, rn_ref, wuv_ref, cc_ref, cr_ref, o_ref,
                       cbuf, rbuf, sems, opart_s):
    n = pl.program_id(0)
    total = pl.num_programs(0) * MLA_GROUPS

    def group_copies(seq, grp, half):
        copies = []
        for u in range(MLA_GROUP_PAGES):
            page = pt_ref[seq, MLA_GROUP_PAGES * grp + u]
            copies.append(pltpu.make_async_copy(cc_ref.at[page], cbuf.at[half, pl.ds(u * PAGE_SIZE, PAGE_SIZE)],
                                                sems.at[0, half]))
            copies.append(pltpu.make_async_copy(cr_ref.at[page], rbuf.at[half, u], sems.at[1, half]))
        return copies

    @pl.when(n == 0)
    def _():
        for cp in group_copies(0, 0, 0):
            cp.start()

    qls = _stack_heads(ql_ref[0], B_HEADS, B_KV_LORA).astype(BF16)
    qrs = _stack_heads(qr_ref[0], B_HEADS, B_ROPE).astype(BF16)
    lane = lax.broadcasted_iota(jnp.int32, (MLA_SAMPLE_ROWS, LANES), 1)

    def group_partials(c0, half, stats):
        m_all, l_all = stats
        latents, scores = [], []
        for k in range(MLA_GROUP_CHUNKS):
            cb = cbuf[half, k * MLA_CHUNK_KEYS:(k + 1) * MLA_CHUNK_KEYS, :].astype(BF16)
            rt = jnp.concatenate([rbuf[half, k * MLA_CHUNK_PAGES + u] for u in range(MLA_CHUNK_PAGES)],
                                 axis=1).astype(BF16)
            latents.append(cb)
            scores.append(_dot_nt(qls, cb) + _dot(qrs, rt))
        probs = []
        for k in range(MLA_GROUP_CHUNKS):
            s = scores[k] * B_SCALE
            m = jnp.max(s, axis=1, keepdims=True)
            p = jnp.exp(s - m)
            probs.append(p.astype(BF16))
            m_all = _set_lane(m_all, lane, c0 + k, m)
            l_all = _set_lane(l_all, lane, c0 + k, jnp.sum(p, axis=1, keepdims=True))
        for k in range(MLA_GROUP_CHUNKS):
            opart_s[c0 + k] = _dot(probs[k], latents[k])
        return m_all, l_all

    def ring_step(jj, stats):
        for half in range(N_HALVES):
            grp = jj * N_HALVES + half
            nxt = n * MLA_GROUPS + grp + 1

            @pl.when(nxt < total)
            def _():
                for cp in group_copies(lax.shift_right_logical(nxt, _log2(MLA_GROUPS)),
                                       nxt & (MLA_GROUPS - 1), (half + 1) % N_HALVES):
                    cp.start()

            for cp in group_copies(n, grp, half):
                cp.wait()
            stats = group_partials(grp * MLA_GROUP_CHUNKS, half, stats)
        return stats

    zeros = jnp.zeros((MLA_SAMPLE_ROWS, LANES), F32)
    mm, ll = lax.fori_loop(0, MLA_GROUPS // N_HALVES, ring_step, (zeros, zeros))

    cn = cn_ref[0].astype(BF16)
    sn = (_dot_nt(qls, cn) + _dot_nt(qrs, rn_ref[0].astype(BF16))) * B_SCALE
    sn = jnp.where(_causal_new_tokens(sn.shape), sn, NEG_INF)
    m_own = jnp.max(sn, axis=1, keepdims=True)
    p_own = jnp.exp(sn - m_own)
    l_own = jnp.sum(p_own, axis=1, keepdims=True)
    o_own = _dot(p_own.astype(BF16), cn)

    cached = lane < MLA_CHUNKS
    m_all = jnp.maximum(m_own, jnp.max(jnp.where(cached, mm, NEG_INF), axis=1, keepdims=True))
    w = jnp.where(cached, jnp.exp(mm - m_all), 0.0)
    w_own = jnp.exp(m_own - m_all)
    denom = jnp.sum(w * ll, axis=1, keepdims=True) + w_own * l_own
    acc = w_own * o_own
    for c in range(MLA_CHUNKS):
        acc = acc + w[:, c:c + 1] * opart_s[c]
    out = (acc / denom).astype(BF16)
    o_ref[0] = jnp.concatenate(
        [_dot(out[h * DEC_SEQ:(h + 1) * DEC_SEQ], wuv_ref[h]) for h in range(B_HEADS)], axis=1)


def _mla_sample(page_table, q_lat, q_rope, ckv_new, kr_new, wuv_t, cache_ckv2, cache_krt):
    assert MLA_GROUPS % N_HALVES == 0 and MLA_CHUNKS <= LANES

    def seq_spec(width):
        return pl.BlockSpec((1, DEC_SEQ, width), lambda n, pt: (n, 0, 0))

    grid_spec = pltpu.PrefetchScalarGridSpec(
        num_scalar_prefetch=1,
        grid=(DEC_BATCH,),
        in_specs=[seq_spec(B_HEADS * B_KV_LORA), seq_spec(B_HEADS * B_ROPE), seq_spec(B_KV_LORA),
                  seq_spec(B_ROPE),
                  pl.BlockSpec(wuv_t.shape, lambda n, pt: (0, 0, 0), pipeline_mode=pl.Buffered(1)),
                  pl.BlockSpec(memory_space=pl.ANY), pl.BlockSpec(memory_space=pl.ANY)],
        out_specs=seq_spec(B_WIDTH),
        scratch_shapes=[pltpu.VMEM((N_HALVES, MLA_GROUP_PAGES * PAGE_SIZE, B_KV_LORA), F32),
                        pltpu.VMEM((N_HALVES, MLA_GROUP_PAGES, B_ROPE, PAGE_SIZE), F32),
                        pltpu.SemaphoreType.DMA((2, N_HALVES)),
                        pltpu.VMEM((MLA_CHUNKS, MLA_SAMPLE_ROWS, B_KV_LORA), F32)])
    return pl.pallas_call(
        _mla_sample_kernel,
        grid_spec=grid_spec,
        out_shape=jax.ShapeDtypeStruct((DEC_BATCH, DEC_SEQ, B_WIDTH), F32),
        compiler_params=_params(("arbitrary",)),
        name="mla_sample",
    )(page_table, q_lat, q_rope, ckv_new, kr_new, wuv_t, cache_ckv2, cache_krt)


def _out_proj_kernel(oa_ref, ga_ref, ob_ref, gb_ref, x_ref, gate_ref, w_ref, lng_ref, lnb_ref, y_ref,
                     *, per_row_mod):
    u = jnp.concatenate([oa_ref[...] * _silu(ga_ref[...]), ob_ref[...] * _silu(gb_ref[...])], axis=1)
    o = _dot(u.astype(BF16), w_ref[...])
    gate = gate_ref[...] if per_row_mod else gate_ref[0]
    r = ALPHA * x_ref[...] + gate * o
    mu = jnp.mean(r, axis=-1, keepdims=True)
    d = r - mu
    var = jnp.mean(d * d, axis=-1, keepdims=True)
    y_ref[...] = d * lax.rsqrt(var + LN_EPS) * lng_ref[...] + lnb_ref[...]


def _out_proj(oa, ga, ob, gb, x, mod, w_out, ln_g, ln_b, *, per_row_mod, rows_per_batch):
    rows = x.shape[0]
    tm = 256
    if per_row_mod:
        gate_spec = pl.BlockSpec((tm, D_MODEL), lambda i: (i, 2))
    else:
        tiles_per_batch = rows_per_batch // tm
        gate_spec = pl.BlockSpec((1, 1, D_MODEL), lambda i: (i // tiles_per_batch, 0, 2))
    half = pl.BlockSpec((tm, A_WIDTH), lambda i: (i, 0))
    full = pl.BlockSpec((tm, D_MODEL), lambda i: (i, 0))
    return pl.pallas_call(
        functools.partial(_out_proj_kernel, per_row_mod=per_row_mod),
        grid=(rows // tm,),
        in_specs=[half, half, half, half, full, gate_spec,
                  _resident(w_out.shape), _resident(ln_g.shape), _resident(ln_b.shape)],
        out_specs=full,
        out_shape=jax.ShapeDtypeStruct((rows, D_MODEL), F32),
        compiler_params=_params(("arbitrary",)),
        name="out_proj_sample" if per_row_mod else "out_proj_prompt",
    )(oa, ga, ob, gb, x, mod, w_out, ln_g, ln_b)


def _rope_tables(pos, half):
    inv = jnp.power(ROPE_THETA, -jnp.arange(half, dtype=F32) / half)
    ang = pos.astype(F32)[:, None] * inv[None, :]
    cos, sin = jnp.cos(ang), jnp.sin(ang)
    return jnp.concatenate([cos, cos], axis=1), jnp.concatenate([-sin, sin], axis=1)


def _rope_tables_128(pos):
    cosa, sina = _rope_tables(pos, A_HEAD_DIM // 2)
    cosb, sinb = _rope_tables(pos, B_ROPE // 2)
    return cosa, sina, jnp.tile(cosb, (1, 2)), jnp.tile(sinb, (1, 2))


def kernel(x_prompt, x_sample, cache_k, cache_v, cache_ckv, cache_kr, page_table, c_prompt, c_sample,
           w_ada, b_ada, w_in, q_norm_g, w_uq, kv_norm_g, w_uk, w_uv, w_out, ln_g, ln_b):
    layer = 0
    n_pool = cache_k.shape[1]
    rows_p = BATCH * SEQ
    rows_s = DEC_BATCH * DEC_SEQ

    w = w_in[layer]
    w_p = jnp.concatenate([w[:, :3840], w[:, 3904:], w[:, 3840:3904], jnp.zeros((D_MODEL, 64), F32)],
                          axis=1).astype(BF16)
    wuq = w_uq[layer].reshape(B_Q_LORA, B_HEADS, B_NOPE + B_ROPE)
    wuq_n = wuq[:, :, :B_NOPE].reshape(B_Q_LORA, B_HEADS * B_NOPE).astype(BF16)
    wuq_r = wuq[:, :, B_NOPE:].reshape(B_Q_LORA, B_HEADS * B_ROPE).astype(BF16)
    wuk_t = jnp.transpose(w_uk[layer], (1, 2, 0)).astype(BF16)
    wuv_t = jnp.transpose(w_uv[layer], (1, 0, 2)).astype(BF16)
    w_o = w_out[layer].astype(BF16)
    qg, kvg, lng, lnb = q_norm_g[layer][None], kv_norm_g[layer][None], ln_g[layer][None], ln_b[layer][None]

    n_c = BATCH + DEC_BATCH
    pad = (-n_c) % 8
    c_all = jnp.concatenate([c_prompt, c_sample, jnp.zeros((pad, D_MODEL), F32)], axis=0)
    mod = _ada_mod(c_all, w_ada[layer], b_ada[layer][None])
    mod_p = mod[:BATCH].reshape(BATCH, 1, 3 * D_MODEL)
    mod_s = jnp.repeat(mod[BATCH:n_c], DEC_SEQ, axis=0)

    tabs_p = _rope_tables_128(jnp.arange(SEQ, dtype=jnp.int32))
    tm = 256
    tabs_s = tuple(jnp.tile(t, (tm // DEC_SEQ, 1))
                   for t in _rope_tables_128(PAST_LEN + jnp.arange(DEC_SEQ, dtype=jnp.int32)))

    xp = x_prompt.reshape(rows_p, D_MODEL)
    xs = x_sample.reshape(rows_s, D_MODEL)
    proj_w = (w_p, wuq_n, wuq_r, wuk_t, qg, kvg)
    qa_p, ka_p, va_p, ga_p, ql_p, qr_p, ckv_p, kr_p, gb_p = _in_proj(
        xp, mod_p, *proj_w, tabs_p, per_row_mod=False, rows_per_batch=SEQ)
    qa_s, ka_s, va_s, ga_s, ql_s, qr_s, ckv_s, kr_s, gb_s = _in_proj(
        xs, mod_s, *proj_w, tabs_s, per_row_mod=True, rows_per_batch=DEC_SEQ)

    oa_p = _moba_prompt(qa_p.reshape(BATCH, SEQ, A_WIDTH), ka_p.reshape(BATCH, SEQ, A_KV_WIDTH),
                        va_p.reshape(BATCH, SEQ, A_KV_WIDTH))
    ob_p = _mla_prompt(ql_p.reshape(BATCH, SEQ, -1), qr_p.reshape(BATCH, SEQ, -1),
                       ckv_p.reshape(BATCH, SEQ, B_KV_LORA), kr_p.reshape(BATCH, SEQ, B_ROPE), wuv_t)
    y_p = _out_proj(oa_p.reshape(rows_p, A_WIDTH), ga_p, ob_p.reshape(rows_p, B_WIDTH), gb_p, xp, mod_p,
                    w_o, lng, lnb, per_row_mod=False, rows_per_batch=SEQ)

    cache_k2 = cache_k[layer].reshape(n_pool, PAGE_ROWS, A_HEAD_DIM)
    cache_v2 = cache_v[layer].reshape(n_pool, PAGE_ROWS, A_HEAD_DIM)
    cache_krt = jnp.swapaxes(cache_kr[layer], 1, 2)
    oa_s = _moba_sample(page_table, qa_s.reshape(DEC_BATCH, DEC_SEQ, A_WIDTH),
                        ka_s.reshape(DEC_BATCH, DEC_SEQ, A_KV_WIDTH), va_s.reshape(DEC_BATCH, DEC_SEQ, A_KV_WIDTH),
                        cache_k2, cache_v2)
    ob_s = _mla_sample(page_table, ql_s.reshape(DEC_BATCH, DEC_SEQ, -1), qr_s.reshape(DEC_BATCH, DEC_SEQ, -1),
                       ckv_s.reshape(DEC_BATCH, DEC_SEQ, B_KV_LORA), kr_s.reshape(DEC_BATCH, DEC_SEQ, B_ROPE),
                       wuv_t, cache_ckv[layer], cache_krt)
    y_s = _out_proj(oa_s.reshape(rows_s, A_WIDTH), ga_s, ob_s.reshape(rows_s, B_WIDTH), gb_s, xs, mod_s,
                    w_o, lng, lnb, per_row_mod=True, rows_per_batch=DEC_SEQ)

    return (y_p.reshape(BATCH, SEQ, D_MODEL), y_s.reshape(DEC_BATCH, DEC_SEQ, D_MODEL),
            ka_p.reshape(DEPTH, BATCH, SEQ, A_KV_HEADS, A_HEAD_DIM),
            va_p.reshape(DEPTH, BATCH, SEQ, A_KV_HEADS, A_HEAD_DIM),
            ckv_p.reshape(DEPTH, BATCH, SEQ, B_KV_LORA), kr_p.reshape(DEPTH, BATCH, SEQ, B_ROPE),
            ka_s.reshape(DEPTH, DEC_BATCH, DEC_SEQ, A_KV_HEADS, A_HEAD_DIM),
            va_s.reshape(DEPTH, DEC_BATCH, DEC_SEQ, A_KV_HEADS, A_HEAD_DIM),
            ckv_s.reshape(DEPTH, DEC_BATCH, DEC_SEQ, B_KV_LORA), kr_s.reshape(DEPTH, DEC_BATCH, DEC_SEQ, B_ROPE))
```

```python
import functools

import jax
import jax.numpy as jnp
from jax import lax
from jax.experimental import pallas as pl
from jax.experimental.pallas import tpu as pltpu

F32 = jnp.float32
BF16 = jnp.bfloat16
NEG_INF = float("-inf")

D_MODEL = 2048
BATCH = 4
SEQ = 2048
DEC_BATCH = 128
DEC_SEQ = 8
PAST_LEN = 8192
PAGE_SIZE = 128
N_PAGES = PAST_LEN // PAGE_SIZE

A_HEAD_DIM = 128
A_HEADS = 8
A_KV_HEADS = 4
A_GROUP = A_HEADS // A_KV_HEADS
A_WIDTH = A_HEADS * A_HEAD_DIM
A_KV_WIDTH = A_KV_HEADS * A_HEAD_DIM
A_BLOCK = 256
A_TOPK = 3
A_SCALE = A_HEAD_DIM ** -0.5
B_NOPE = 128
B_ROPE = 64
B_VDIM = 128
B_HEADS = 8
B_WIDTH = B_HEADS * B_VDIM
B_Q_LORA = 512
B_KV_LORA = 256
B_SCALE = (B_NOPE + B_ROPE) ** -0.5
ROPE_THETA = 10000.0
RMS_EPS = 1e-6
LN_EPS = 1e-5
DEPTH = 1
ALPHA = (2 * DEPTH) ** 0.25

N_BLOCKS_PROMPT = SEQ // A_BLOCK
N_BLOCKS_PAST = PAST_LEN // A_BLOCK
PAGES_PER_BLOCK = A_BLOCK // PAGE_SIZE

C_QA = (0, 1024)
C_KA = (1024, 1536)
C_VA = (1536, 2048)
C_GA = (2048, 3072)
C_CQ = (3072, 3584)
C_CKV = (3584, 3840)
C_GB = (3840, 4864)
C_KR = (4864, 4992)
W_IN_COLS = 4992

LANES = 128
VMEM_LIMIT_BYTES = 56 * 1024 * 1024


def _params(semantics):
    return pltpu.CompilerParams(dimension_semantics=semantics, vmem_limit_bytes=VMEM_LIMIT_BYTES)


def _silu(x):
    return x * jax.nn.sigmoid(x)


def _dot(a, b):
    return jnp.dot(a, b, preferred_element_type=F32)


def _dot_nt(a, b):
    return lax.dot_general(a, b, (((1,), (1,)), ((), ())), preferred_element_type=F32)


def _resident(shape):
    nd = len(shape)
    return pl.BlockSpec(shape, lambda *_: (0,) * nd, pipeline_mode=pl.Buffered(1))


def _ada_kernel(c_ref, w_ref, b_ref, o_ref):
    a = _silu(c_ref[...]).astype(BF16)
    o_ref[...] = _dot(a, w_ref[...].astype(BF16)) + b_ref[...]


def _ada_mod(c_all, w_ada, b_ada):
    rows = c_all.shape[0]
    n = w_ada.shape[1]
    tn = 768
    return pl.pallas_call(
        _ada_kernel,
        grid=(n // tn,),
        in_specs=[pl.BlockSpec((rows, D_MODEL), lambda j: (0, 0)),
                  pl.BlockSpec((D_MODEL, tn), lambda j: (0, j)),
                  pl.BlockSpec((1, tn), lambda j: (0, j))],
        out_specs=pl.BlockSpec((rows, tn), lambda j: (0, j)),
        out_shape=jax.ShapeDtypeStruct((rows, n), F32),
        compiler_params=_params(("arbitrary",)),
        name="ada_mod",
    )(c_all, w_ada, b_ada)


def _in_proj_kernel(x_ref, shift_ref, scale_ref, w_ref, wuqn_ref, wuqr_ref, wuk_ref, qg_ref, kvg_ref,
                    cosa_ref, sina_ref, cosb_ref, sinb_ref,
                    qa_ref, ka_ref, va_ref, ga_ref, qlat_ref, qrope_ref, ckv_ref, kr_ref, gb_ref,
                    *, per_row_mod):
    x = x_ref[...]
    if per_row_mod:
        shift, scale = shift_ref[...], scale_ref[...]
    else:
        shift, scale = shift_ref[0], scale_ref[0]
    hb = (x * (1.0 + scale) + shift).astype(BF16)
    tm = x.shape[0]
    cosa, sina = cosa_ref[...], sina_ref[...]
    cosb, sinb = cosb_ref[...], sinb_ref[...]

    def proj(cols):
        return _dot(hb, w_ref[:, cols[0]:cols[1]])

    def rope_full(z, n_heads):
        outs = []
        for h in range(n_heads):
            zh = z[:, h * LANES:(h + 1) * LANES]
            outs.append(zh * cosa + pltpu.roll(zh, 64, 1) * sina)
        return jnp.concatenate(outs, axis=1)

    lane = lax.broadcasted_iota(jnp.int32, (tm, LANES), 1)
    first_half = (lane & 63) < 32

    def rope_half(zg):
        partner = jnp.where(first_half, pltpu.roll(zg, 96, 1), pltpu.roll(zg, 32, 1))
        return zg * cosb + partner * sinb

    def rms(z, g):
        return z * lax.rsqrt(jnp.mean(z * z, axis=-1, keepdims=True) + RMS_EPS) * g

    qa_ref[...] = rope_full(proj(C_QA), A_HEADS)
    ka_ref[...] = rope_full(proj(C_KA), A_KV_HEADS)
    va_ref[...] = proj(C_VA)
    ga_ref[...] = proj(C_GA)
    gb_ref[...] = proj(C_GB)
    ckv_ref[...] = rms(proj(C_CKV), kvg_ref[...])
    kr_ref[...] = rope_half(proj(C_KR))[:, :B_ROPE]

    cqn = rms(proj(C_CQ), qg_ref[...]).astype(BF16)
    qn = _dot(cqn, wuqn_ref[...])
    qr = _dot(cqn, wuqr_ref[...])
    for h in range(B_HEADS):
        qlat_ref[:, h * B_KV_LORA:(h + 1) * B_KV_LORA] = _dot(
            qn[:, h * B_NOPE:(h + 1) * B_NOPE].astype(BF16), wuk_ref[h])
    qrope_ref[...] = jnp.concatenate(
        [rope_half(qr[:, g * LANES:(g + 1) * LANES]) for g in range(B_HEADS * B_ROPE // LANES)], axis=1)


def _in_proj(x, mod, w_p, wuq_n, wuq_r, wuk_t, q_norm_g, kv_norm_g, tabs, *, per_row_mod, rows_per_batch):
    rows = x.shape[0]
    tm = 256
    steps = rows // tm
    if per_row_mod:
        shift_spec = pl.BlockSpec((tm, D_MODEL), lambda i: (i, 0))
        scale_spec = pl.BlockSpec((tm, D_MODEL), lambda i: (i, 1))
        tab_spec = pl.BlockSpec((tm, LANES), lambda i: (0, 0))
    else:
        tiles_per_batch = rows_per_batch // tm
        shift_spec = pl.BlockSpec((1, 1, D_MODEL), lambda i: (i // tiles_per_batch, 0, 0))
        scale_spec = pl.BlockSpec((1, 1, D_MODEL), lambda i: (i // tiles_per_batch, 0, 1))
        tab_spec = pl.BlockSpec((tm, LANES), lambda i: (i % tiles_per_batch, 0))
    widths = (A_WIDTH, A_KV_WIDTH, A_KV_WIDTH, A_WIDTH, B_HEADS * B_KV_LORA, B_HEADS * B_ROPE,
              B_KV_LORA, B_ROPE, B_WIDTH)
    return pl.pallas_call(
        functools.partial(_in_proj_kernel, per_row_mod=per_row_mod),
        grid=(steps,),
        in_specs=[pl.BlockSpec((tm, D_MODEL), lambda i: (i, 0)), shift_spec, scale_spec,
                  _resident(w_p.shape), _resident(wuq_n.shape), _resident(wuq_r.shape),
                  _resident(wuk_t.shape), _resident(q_norm_g.shape), _resident(kv_norm_g.shape),
                  tab_spec, tab_spec, tab_spec, tab_spec],
        out_specs=[pl.BlockSpec((tm, w), lambda i: (i, 0)) for w in widths],
        out_shape=[jax.ShapeDtypeStruct((rows, w), F32) for w in widths],
        compiler_params=_params(("arbitrary",)),
        name="in_proj_sample" if per_row_mod else "in_proj_prompt",
    )(x, mod, mod, w_p, wuq_n, wuq_r, wuk_t, q_norm_g, kv_norm_g, *tabs)


MOBA_ROW_CHUNK = 128


def _lanes2(x):
    return jnp.concatenate([x, x], axis=1)


ONES_ROWS = 16
VT_ROWS = A_HEAD_DIM + ONES_ROWS


def _moba_prompt_kernel(q_ref, k_ref, v_ref, o_ref, kb_s, vt_s, cb_s):
    k = k_ref[0]
    kb_s[...] = k.astype(BF16)
    vt_s[:A_HEAD_DIM, :] = v_ref[0].T.astype(BF16)
    vt_s[A_HEAD_DIM:, :] = jnp.ones((ONES_ROWS, SEQ), BF16)
    km = jnp.concatenate(
        [jnp.sum(k[j * A_BLOCK:(j + 1) * A_BLOCK], axis=0, keepdims=True) * (1.0 / A_BLOCK)
         for j in range(N_BLOCKS_PROMPT)], axis=0).astype(BF16)

    rows = A_GROUP * A_BLOCK
    sub = 8
    tiles = A_BLOCK // sub
    key = lax.broadcasted_iota(jnp.int32, (A_BLOCK, rows), 0)
    tok = lax.broadcasted_iota(jnp.int32, (A_BLOCK, rows), 1) & (A_BLOCK - 1)
    cb_s[...] = jnp.where(key <= tok, 0.0, NEG_INF)
    blk = lax.broadcasted_iota(jnp.int32, (N_BLOCKS_PROMPT, rows), 0)

    for i in range(N_BLOCKS_PROMPT):
        q2 = q_ref[0, i * A_BLOCK:(i + 1) * A_BLOCK, :]
        qb = jnp.concatenate([q2[:, g * A_HEAD_DIM:(g + 1) * A_HEAD_DIM] for g in range(A_GROUP)],
                             axis=0).astype(BF16)
        n_keys = (i + 1) * A_BLOCK
        s = _dot_nt(kb_s[:n_keys, :], qb) * A_SCALE
        s3 = s.reshape(n_keys // sub, sub, rows)
        pieces = []
        if i > 0:
            sg = _dot_nt(km, qb)
            rank = jnp.zeros(sg.shape, jnp.int32)
            for jp in range(i):
                row = sg[jp:jp + 1, :]
                beats = (row > sg) | ((row == sg) & (blk > jp))
                rank = rank + jnp.where(beats, 1, 0)
            gate_bias = jnp.where(rank < A_TOPK, 0.0, NEG_INF)
            for j in range(i):
                bias_j = jnp.broadcast_to(gate_bias[j:j + 1, :], (sub, rows))
                pieces.append(s3[j * tiles:(j + 1) * tiles] + bias_j[None])
        pieces.append(s3[i * tiles:] + cb_s[...].reshape(tiles, sub, rows))
        s3 = jnp.concatenate(pieces, axis=0)
        m = jnp.max(s3, axis=0)
        for shift in (4, 2, 1):
            m = jnp.maximum(m, pltpu.roll(m, shift, 0))
        p = jnp.exp(s3 - m[None]).reshape(n_keys, rows).astype(BF16)
        acc = _dot(vt_s[:, :n_keys], p)
        denom = acc[A_HEAD_DIM:A_HEAD_DIM + sub]
        out_t = (acc[:A_HEAD_DIM].reshape(A_HEAD_DIM // sub, sub, rows) / denom[None]).reshape(A_HEAD_DIM, rows)
        o_ref[0, i * A_BLOCK:(i + 1) * A_BLOCK, :] = jnp.concatenate(
            [out_t[:, g * A_BLOCK:(g + 1) * A_BLOCK].T for g in range(A_GROUP)], axis=1)


def _moba_prompt(qa, ka, va):
    gw = A_GROUP * A_HEAD_DIM
    return pl.pallas_call(
        _moba_prompt_kernel,
        grid=(BATCH, A_KV_HEADS),
        in_specs=[pl.BlockSpec((1, SEQ, gw), lambda b, g: (b, 0, g)),
                  pl.BlockSpec((1, SEQ, A_HEAD_DIM), lambda b, g: (b, 0, g)),
                  pl.BlockSpec((1, SEQ, A_HEAD_DIM), lambda b, g: (b, 0, g))],
        out_specs=pl.BlockSpec((1, SEQ, gw), lambda b, g: (b, 0, g)),
        out_shape=jax.ShapeDtypeStruct((BATCH, SEQ, A_WIDTH), F32),
        scratch_shapes=[pltpu.VMEM((SEQ, A_HEAD_DIM), BF16), pltpu.VMEM((VT_ROWS, SEQ), BF16),
                        pltpu.VMEM((A_BLOCK, A_GROUP * A_BLOCK), F32)],
        compiler_params=_params(("arbitrary", "arbitrary")),
        name="moba_prompt",
    )(qa, ka, va)


MLA_TQ = 128
MLA_TK = 256
MLA_ROW_CHUNK = 256


def _mla_prompt_kernel(ql_ref, qr_ref, ckv_ref, kr_ref, wuv_ref, o_ref,
                       ckvb_s, krb_s, qls_s, qrs_s, m_s, l_s, acc_s):
    i = pl.program_id(1)
    tq = MLA_TQ
    rc = MLA_ROW_CHUNK
    chunks = [slice(c * rc, (c + 1) * rc) for c in range(B_HEADS * tq // rc)]

    @pl.when(i == 0)
    def _():
        ckvb_s[...] = ckv_ref[0].astype(BF16)
        krb_s[...] = kr_ref[0].astype(BF16)

    ql = ql_ref[0]
    qr = qr_ref[0]
    for h in range(B_HEADS):
        qls_s[h * tq:(h + 1) * tq, :] = ql[:, h * B_KV_LORA:(h + 1) * B_KV_LORA].astype(BF16)
        qrs_s[h * tq:(h + 1) * tq, :] = qr[:, h * B_ROPE:(h + 1) * B_ROPE].astype(BF16)

    def scores(rs, ckv_blk, kr_blk):
        return (_dot_nt(qls_s[rs, :], ckv_blk) + _dot_nt(qrs_s[rs, :], kr_blk)) * B_SCALE

    jd = (i * tq) // MLA_TK
    dstart = pl.multiple_of(jd * MLA_TK, MLA_TK)
    ckv_d = ckvb_s[pl.ds(dstart, MLA_TK), :]
    kr_d = krb_s[pl.ds(dstart, MLA_TK), :]
    sds = [scores(rs, ckv_d, kr_d) for rs in chunks]
    pds = []
    for c, (rs, s) in enumerate(zip(chunks, sds)):
        qpos = i * tq + ((c * rc + lax.broadcasted_iota(jnp.int32, s.shape, 0)) & (tq - 1))
        kpos = jd * MLA_TK + lax.broadcasted_iota(jnp.int32, s.shape, 1)
        s = jnp.where(qpos >= kpos, s, NEG_INF)
        m = jnp.broadcast_to(jnp.max(s, axis=1, keepdims=True), (rc, LANES))
        p = jnp.exp(s - _lanes2(m))
        m_s[rs, :] = m
        l_s[rs, :] = jnp.broadcast_to(jnp.sum(p, axis=1, keepdims=True), (rc, LANES))
        pds.append(p.astype(BF16))
    for rs, p in zip(chunks, pds):
        acc_s[rs, :] = _dot(p, ckv_d)

    def body(j, carry):
        start = pl.multiple_of(j * MLA_TK, MLA_TK)
        ckv_j = ckvb_s[pl.ds(start, MLA_TK), :]
        kr_j = krb_s[pl.ds(start, MLA_TK), :]
        sjs = [scores(rs, ckv_j, kr_j) for rs in chunks]
        pjs, alphas = [], []
        for rs, sj in zip(chunks, sjs):
            m_old = m_s[rs, :]
            m_new = jnp.maximum(m_old, jnp.max(sj, axis=1, keepdims=True))
            alpha = jnp.exp(m_old - m_new)
            pj = jnp.exp(sj - _lanes2(m_new))
            l_s[rs, :] = alpha * l_s[rs, :] + jnp.sum(pj, axis=1, keepdims=True)
            m_s[rs, :] = m_new
            pjs.append(pj.astype(BF16))
            alphas.append(alpha)
        for rs, pj, alpha in zip(chunks, pjs, alphas):
            acc_s[rs, :] = _lanes2(alpha) * acc_s[rs, :] + _dot(pj, ckv_j)
        return carry

    lax.fori_loop(0, jd, body, 0)

    out = (acc_s[...] / _lanes2(l_s[...])).astype(BF16)
    o_ref[0] = jnp.concatenate([_dot(out[h * tq:(h + 1) * tq], wuv_ref[h]) for h in range(B_HEADS)], axis=1)


def _mla_prompt(q_lat, q_rope, ckv, kr, wuv_t):
    tq = MLA_TQ
    rows = B_HEADS * tq
    return pl.pallas_call(
        _mla_prompt_kernel,
        grid=(BATCH, SEQ // tq),
        in_specs=[pl.BlockSpec((1, tq, B_HEADS * B_KV_LORA), lambda b, i: (b, i, 0)),
                  pl.BlockSpec((1, tq, B_HEADS * B_ROPE), lambda b, i: (b, i, 0)),
                  pl.BlockSpec((1, SEQ, B_KV_LORA), lambda b, i: (b, 0, 0)),
                  pl.BlockSpec((1, SEQ, B_ROPE), lambda b, i: (b, 0, 0)),
                  _resident(wuv_t.shape)],
        out_specs=pl.BlockSpec((1, tq, B_WIDTH), lambda b, i: (b, i, 0)),
        out_shape=jax.ShapeDtypeStruct((BATCH, SEQ, B_WIDTH), F32),
        scratch_shapes=[pltpu.VMEM((SEQ, B_KV_LORA), BF16), pltpu.VMEM((SEQ, B_ROPE), BF16),
                        pltpu.VMEM((rows, B_KV_LORA), BF16), pltpu.VMEM((rows, B_ROPE), BF16),
                        pltpu.VMEM((rows, LANES), F32), pltpu.VMEM((rows, LANES), F32),
                        pltpu.VMEM((rows, B_KV_LORA), F32)],
        compiler_params=_params(("arbitrary", "arbitrary")),
        name="mla_prompt",
    )(q_lat, q_rope, ckv, kr, wuv_t)


SAMPLE_ROWS = A_HEADS * DEC_SEQ
KV_ROWS = A_GROUP * DEC_SEQ


def _stack_heads(x, n_heads, width):
    return jnp.concatenate([x[:, h * width:(h + 1) * width] for h in range(n_heads)], axis=0)


def _unstack_heads(x, n_heads, tokens):
    return jnp.concatenate([x[h * tokens:(h + 1) * tokens] for h in range(n_heads)], axis=1)


def _causal_new_tokens(shape):
    tq = lax.broadcasted_iota(jnp.int32, shape, 0) & (DEC_SEQ - 1)
    tk = lax.broadcasted_iota(jnp.int32, shape, 1)
    return tk <= tq


def _log2(n):
    assert n & (n - 1) == 0
    return n.bit_length() - 1


def _set_lane(acc, lane, idx, col):
    return jnp.where(lane == idx, col, acc)


MOBA_GROUP = 4
MOBA_GROUPS = N_BLOCKS_PAST // MOBA_GROUP
PAGE_ROWS = PAGE_SIZE * A_KV_HEADS
BLOCK_ROWS = A_BLOCK * A_KV_HEADS
N_HALVES = 2


def _moba_sample_kernel(pt_ref, q_ref, kn_ref, vn_ref, ck_ref, cv_ref, o_ref, kbuf, vbuf, sems, opart_s):
    n = pl.program_id(0)
    total = pl.num_programs(0) * MOBA_GROUPS

    def group_copies(seq, grp, half):
        copies = []
        for u in range(MOBA_GROUP * PAGES_PER_BLOCK):
            page = pt_ref[seq, MOBA_GROUP * PAGES_PER_BLOCK * grp + u]
            rows = pl.ds(u * PAGE_ROWS, PAGE_ROWS)
            copies.append((pltpu.make_async_copy(ck_ref.at[page], kbuf.at[half, rows], sems.at[0, half]), 0))
            copies.append((pltpu.make_async_copy(cv_ref.at[page], vbuf.at[half, rows], sems.at[1, half]), 1))
        return copies

    @pl.when(n == 0)
    def _():
        for cp, prio in group_copies(0, 0, 0):
            cp.start(priority=prio)

    qf = _stack_heads(q_ref[0], A_HEADS, A_HEAD_DIM)
    qb = qf.astype(BF16)
    row_head = lax.shift_right_logical(lax.broadcasted_iota(jnp.int32, (SAMPLE_ROWS, BLOCK_ROWS), 0),
                                       _log2(KV_ROWS))
    col_head = lax.broadcasted_iota(jnp.int32, (SAMPLE_ROWS, BLOCK_ROWS), 1) & (A_KV_HEADS - 1)
    same_head = row_head == col_head
    lane = lax.broadcasted_iota(jnp.int32, (SAMPLE_ROWS, LANES), 1)

    def group_partials(j0, half, stats):
        m_all, l_all, sg_all = stats
        block_rows = [slice(b * BLOCK_ROWS, (b + 1) * BLOCK_ROWS) for b in range(MOBA_GROUP)]
        scores, gates = [], []
        for rows in block_rows:
            kf = kbuf[half, rows, :]
            ks = jnp.sum(kf.reshape(BLOCK_ROWS // 8, 8, A_HEAD_DIM), axis=0)
            km = (ks[:A_KV_HEADS] + ks[A_KV_HEADS:]) * (1.0 / A_BLOCK)
            km_rows = jnp.concatenate(
                [jnp.broadcast_to(km[g:g + 1], (KV_ROWS, A_HEAD_DIM)) for g in range(A_KV_HEADS)], axis=0)
            gates.append(jnp.sum(qf * km_rows, axis=1, keepdims=True))
            scores.append(_dot_nt(qb, kf.astype(BF16)))
        probs = []
        for b in range(MOBA_GROUP):
            s = jnp.where(same_head, scores[b] * A_SCALE, NEG_INF)
            m = jnp.max(s, axis=1, keepdims=True)
            p = jnp.exp(s - m)
            probs.append(p.astype(BF16))
            m_all = _set_lane(m_all, lane, j0 + b, m)
            l_all = _set_lane(l_all, lane, j0 + b, jnp.sum(p, axis=1, keepdims=True))
            sg_all = _set_lane(sg_all, lane, j0 + b, gates[b])
        for b, rows in enumerate(block_rows):
            opart_s[j0 + b] = _dot(probs[b], vbuf[half, rows, :].astype(BF16))
        return m_all, l_all, sg_all

    def ring_step(jj, stats):
        for half in range(N_HALVES):
            grp = jj * N_HALVES + half
            nxt = n * MOBA_GROUPS + grp + 1

            @pl.when(nxt < total)
            def _():
                for cp, prio in group_copies(lax.shift_right_logical(nxt, _log2(MOBA_GROUPS)),
                                             nxt & (MOBA_GROUPS - 1), (half + 1) % N_HALVES):
                    cp.start(priority=prio)

            for cp, _ in group_copies(n, grp, half):
                cp.wait()
            stats = group_partials(grp * MOBA_GROUP, half, stats)
        return stats

    zeros = jnp.zeros((SAMPLE_ROWS, LANES), F32)
    mm, ll, sg = lax.fori_loop(0, MOBA_GROUPS // N_HALVES, ring_step, (zeros, zeros, zeros))

    rank = jnp.zeros(sg.shape, jnp.int32)
    for jp in range(N_BLOCKS_PAST):
        col = sg[:, jp:jp + 1]
        beats = (col > sg) | ((col == sg) & (lane > jp))
        rank = rank + jnp.where(beats, 1, 0)
    sel = (rank < A_TOPK) & (lane < N_BLOCKS_PAST)

    kn, vn = kn_ref[0], vn_ref[0]
    mo, lo, oo = [], [], []
    for g in range(A_KV_HEADS):
        rs = slice(g * KV_ROWS, (g + 1) * KV_ROWS)
        cs = slice(g * A_HEAD_DIM, (g + 1) * A_HEAD_DIM)
        s = _dot_nt(qb[rs], kn[:, cs].astype(BF16)) * A_SCALE
        s = jnp.where(_causal_new_tokens(s.shape), s, NEG_INF)
        m = jnp.max(s, axis=1, keepdims=True)
        p = jnp.exp(s - m)
        mo.append(m)
        lo.append(jnp.sum(p, axis=1, keepdims=True))
        oo.append(_dot(p.astype(BF16), vn[:, cs].astype(BF16)))
    m_own = jnp.concatenate(mo, axis=0)
    l_own = jnp.concatenate(lo, axis=0)
    o_own = jnp.concatenate(oo, axis=0)

    m_all = jnp.maximum(m_own, jnp.max(jnp.where(sel, mm, NEG_INF), axis=1, keepdims=True))
    w = jnp.where(sel, jnp.exp(mm - m_all), 0.0)
    w_own = jnp.exp(m_own - m_all)
    denom = jnp.sum(w * ll, axis=1, keepdims=True) + w_own * l_own
    acc = w_own * o_own
    for jb in range(N_BLOCKS_PAST):
        acc = acc + w[:, jb:jb + 1] * opart_s[jb]
    o_ref[0] = _unstack_heads(acc / denom, A_HEADS, DEC_SEQ)


def _moba_sample(page_table, qa, ka_new, va_new, cache_k2, cache_v2):
    assert MOBA_GROUPS % N_HALVES == 0

    def seq_spec(width):
        return pl.BlockSpec((1, DEC_SEQ, width), lambda n, pt: (n, 0, 0))

    grid_spec = pltpu.PrefetchScalarGridSpec(
        num_scalar_prefetch=1,
        grid=(DEC_BATCH,),
        in_specs=[seq_spec(A_WIDTH), seq_spec(A_KV_WIDTH), seq_spec(A_KV_WIDTH),
                  pl.BlockSpec(memory_space=pl.ANY), pl.BlockSpec(memory_space=pl.ANY)],
        out_specs=seq_spec(A_WIDTH),
        scratch_shapes=[pltpu.VMEM((N_HALVES, MOBA_GROUP * BLOCK_ROWS, A_HEAD_DIM), F32),
                        pltpu.VMEM((N_HALVES, MOBA_GROUP * BLOCK_ROWS, A_HEAD_DIM), F32),
                        pltpu.SemaphoreType.DMA((2, N_HALVES)),
                        pltpu.VMEM((N_BLOCKS_PAST, SAMPLE_ROWS, A_HEAD_DIM), F32)])
    return pl.pallas_call(
        _moba_sample_kernel,
        grid_spec=grid_spec,
        out_shape=jax.ShapeDtypeStruct((DEC_BATCH, DEC_SEQ, A_WIDTH), F32),
        compiler_params=_params(("arbitrary",)),
        name="moba_sample",
    )(page_table, qa, ka_new, va_new, cache_k2, cache_v2)


MLA_GROUP_PAGES = 16
MLA_GROUPS = N_PAGES // MLA_GROUP_PAGES
MLA_CHUNK_PAGES = 4
MLA_CHUNK_KEYS = MLA_CHUNK_PAGES * PAGE_SIZE
MLA_GROUP_CHUNKS = MLA_GROUP_PAGES // MLA_CHUNK_PAGES
MLA_CHUNKS = N_PAGES // MLA_CHUNK_PAGES
MLA_SAMPLE_ROWS = B_HEADS * DEC_SEQ


def _mla_sample_kernel(pt_ref, ql_ref, qr_ref, cn_ref, rn_ref, wuv_ref, cc_ref, cr_ref, o_ref,
                       cbuf, rbuf, sems, opart_s):
    n = pl.program_id(0)
    total = pl.num_programs(0) * MLA_GROUPS

    def group_copies(seq, grp, half):
        copies = []
        for u in range(MLA_GROUP_PAGES):
            page = pt_ref[seq, MLA_GROUP_PAGES * grp + u]
            copies.append((pltpu.make_async_copy(cc_ref.at[page], cbuf.at[half, pl.ds(u * PAGE_SIZE, PAGE_SIZE)],
                                                 sems.at[0, half]), u % 2))
            copies.append((pltpu.make_async_copy(cr_ref.at[page], rbuf.at[half, u], sems.at[1, half]), (u + 1) % 2))
        return copies

    @pl.when(n == 0)
    def _():
        for cp, prio in group_copies(0, 0, 0):
            cp.start(priority=prio)

    qls = _stack_heads(ql_ref[0], B_HEADS, B_KV_LORA).astype(BF16)
    qrs = _stack_heads(qr_ref[0], B_HEADS, B_ROPE).astype(BF16)
    lane = lax.broadcasted_iota(jnp.int32, (MLA_SAMPLE_ROWS, LANES), 1)

    def group_partials(c0, half, stats):
        m_all, l_all = stats
        latents, scores = [], []
        for k in range(MLA_GROUP_CHUNKS):
            cb = cbuf[half, k * MLA_CHUNK_KEYS:(k + 1) * MLA_CHUNK_KEYS, :].astype(BF16)
            rt = jnp.concatenate([rbuf[half, k * MLA_CHUNK_PAGES + u] for u in range(MLA_CHUNK_PAGES)],
                                 axis=1).astype(BF16)
            latents.append(cb)
            scores.append(_dot_nt(qls, cb) + _dot(qrs, rt))
        probs = []
        for k in range(MLA_GROUP_CHUNKS):
            s = scores[k] * B_SCALE
            m = jnp.max(s, axis=1, keepdims=True)
            p = jnp.exp(s - m)
            probs.append(p.astype(BF16))
            m_all = _set_lane(m_all, lane, c0 + k, m)
            l_all = _set_lane(l_all, lane, c0 + k, jnp.sum(p, axis=1, keepdims=True))
        for k in range(MLA_GROUP_CHUNKS):
            opart_s[c0 + k] = _dot(probs[k], latents[k])
        return m_all, l_all

    def ring_step(jj, stats):
        for half in range(N_HALVES):
            grp = jj * N_HALVES + half
            nxt = n * MLA_GROUPS + grp + 1

            @pl.when(nxt < total)
            def _():
                for cp, prio in group_copies(lax.shift_right_logical(nxt, _log2(MLA_GROUPS)),
                                             nxt & (MLA_GROUPS - 1), (half + 1) % N_HALVES):
                    cp.start(priority=prio)

            for cp, _ in group_copies(n, grp, half):
                cp.wait()
            stats = group_partials(grp * MLA_GROUP_CHUNKS, half, stats)
        return stats

    zeros = jnp.zeros((MLA_SAMPLE_ROWS, LANES), F32)
    mm, ll = lax.fori_loop(0, MLA_GROUPS // N_HALVES, ring_step, (zeros, zeros))

    cn = cn_ref[0].astype(BF16)
    sn = (_dot_nt(qls, cn) + _dot_nt(qrs, rn_ref[0].astype(BF16))) * B_SCALE
    sn = jnp.where(_causal_new_tokens(sn.shape), sn, NEG_INF)
    m_own = jnp.max(sn, axis=1, keepdims=True)
    p_own = jnp.exp(sn - m_own)
    l_own = jnp.sum(p_own, axis=1, keepdims=True)
    o_own = _dot(p_own.astype(BF16), cn)

    cached = lane < MLA_CHUNKS
    m_all = jnp.maximum(m_own, jnp.max(jnp.where(cached, mm, NEG_INF), axis=1, keepdims=True))
    w = jnp.where(cached, jnp.exp(mm - m_all), 0.0)
    w_own = jnp.exp(m_own - m_all)
    denom = jnp.sum(w * ll, axis=1, keepdims=True) + w_own * l_own
    acc = w_own * o_own
    for c in range(MLA_CHUNKS):
        acc = acc + w[:, c:c + 1] * opart_s[c]
    out = (acc / denom).astype(BF16)
    o_ref[0] = jnp.concatenate(
        [_dot(out[h * DEC_SEQ:(h + 1) * DEC_SEQ], wuv_ref[h]) for h in range(B_HEADS)], axis=1)


def _mla_sample(page_table, q_lat, q_rope, ckv_new, kr_new, wuv_t, cache_ckv2, cache_krt):
    assert MLA_GROUPS % N_HALVES == 0 and MLA_CHUNKS <= LANES

    def seq_spec(width):
        return pl.BlockSpec((1, DEC_SEQ, width), lambda n, pt: (n, 0, 0))

    grid_spec = pltpu.PrefetchScalarGridSpec(
        num_scalar_prefetch=1,
        grid=(DEC_BATCH,),
        in_specs=[seq_spec(B_HEADS * B_KV_LORA), seq_spec(B_HEADS * B_ROPE), seq_spec(B_KV_LORA),
                  seq_spec(B_ROPE),
                  pl.BlockSpec(wuv_t.shape, lambda n, pt: (0, 0, 0), pipeline_mode=pl.Buffered(1)),
                  pl.BlockSpec(memory_space=pl.ANY), pl.BlockSpec(memory_space=pl.ANY)],
        out_specs=seq_spec(B_WIDTH),
        scratch_shapes=[pltpu.VMEM((N_HALVES, MLA_GROUP_PAGES * PAGE_SIZE, B_KV_LORA), F32),
                        pltpu.VMEM((N_HALVES, MLA_GROUP_PAGES, B_ROPE, PAGE_SIZE), F32),
                        pltpu.SemaphoreType.DMA((2, N_HALVES)),
                        pltpu.VMEM((MLA_CHUNKS, MLA_SAMPLE_ROWS, B_KV_LORA), F32)])
    return pl.pallas_call(
        _mla_sample_kernel,
        grid_spec=grid_spec,
        out_shape=jax.ShapeDtypeStruct((DEC_BATCH, DEC_SEQ, B_WIDTH), F32),
        compiler_params=_params(("arbitrary",)),
        name="mla_sample",
    )(page_table, q_lat, q_rope, ckv_new, kr_new, wuv_t, cache_ckv2, cache_krt)


def _out_proj_kernel(oa_ref, ga_ref, ob_ref, gb_ref, x_ref, gate_ref, w_ref, lng_ref, lnb_ref, y_ref,
                     *, per_row_mod):
    u = jnp.concatenate([oa_ref[...] * _silu(ga_ref[...]), ob_ref[...] * _silu(gb_ref[...])], axis=1)
    o = _dot(u.astype(BF16), w_ref[...])
    gate = gate_ref[...] if per_row_mod else gate_ref[0]
    r = ALPHA * x_ref[...] + gate * o
    mu = jnp.mean(r, axis=-1, keepdims=True)
    d = r - mu
    var = jnp.mean(d * d, axis=-1, keepdims=True)
    y_ref[...] = d * lax.rsqrt(var + LN_EPS) * lng_ref[...] + lnb_ref[...]


def _out_proj(oa, ga, ob, gb, x, mod, w_out, ln_g, ln_b, *, per_row_mod, rows_per_batch):
    rows = x.shape[0]
    tm = 256
    if per_row_mod:
        gate_spec = pl.BlockSpec((tm, D_MODEL), lambda i: (i, 2))
    else:
        tiles_per_batch = rows_per_batch // tm
        gate_spec = pl.BlockSpec((1, 1, D_MODEL), lambda i: (i // tiles_per_batch, 0, 2))
    half = pl.BlockSpec((tm, A_WIDTH), lambda i: (i, 0))
    full = pl.BlockSpec((tm, D_MODEL), lambda i: (i, 0))
    return pl.pallas_call(
        functools.partial(_out_proj_kernel, per_row_mod=per_row_mod),
        grid=(rows // tm,),
        in_specs=[half, half, half, half, full, gate_spec,
                  _resident(w_out.shape), _resident(ln_g.shape), _resident(ln_b.shape)],
        out_specs=full,
        out_shape=jax.ShapeDtypeStruct((rows, D_MODEL), F32),
        compiler_params=_params(("arbitrary",)),
        name="out_proj_sample" if per_row_mod else "out_proj_prompt",
    )(oa, ga, ob, gb, x, mod, w_out, ln_g, ln_b)


def _rope_tables(pos, half):
    inv = jnp.power(ROPE_THETA, -jnp.arange(half, dtype=F32) / half)
    ang = pos.astype(F32)[:, None] * inv[None, :]
    cos, sin = jnp.cos(ang), jnp.sin(ang)
    return jnp.concatenate([cos, cos], axis=1), jnp.concatenate([-sin, sin], axis=1)


def _rope_tables_128(pos):
    cosa, sina = _rope_tables(pos, A_HEAD_DIM // 2)
    cosb, sinb = _rope_tables(pos, B_ROPE // 2)
    return cosa, sina, jnp.tile(cosb, (1, 2)), jnp.tile(sinb, (1, 2))


def kernel(x_prompt, x_sample, cache_k, cache_v, cache_ckv, cache_kr, page_table, c_prompt, c_sample,
           w_ada, b_ada, w_in, q_norm_g, w_uq, kv_norm_g, w_uk, w_uv, w_out, ln_g, ln_b):
    layer = 0
    n_pool = cache_k.shape[1]
    rows_p = BATCH * SEQ
    rows_s = DEC_BATCH * DEC_SEQ

    w = w_in[layer]
    w_p = jnp.concatenate([w[:, :3840], w[:, 3904:], w[:, 3840:3904], jnp.zeros((D_MODEL, 64), F32)],
                          axis=1).astype(BF16)
    wuq = w_uq[layer].reshape(B_Q_LORA, B_HEADS, B_NOPE + B_ROPE)
    wuq_n = wuq[:, :, :B_NOPE].reshape(B_Q_LORA, B_HEADS * B_NOPE).astype(BF16)
    wuq_r = wuq[:, :, B_NOPE:].reshape(B_Q_LORA, B_HEADS * B_ROPE).astype(BF16)
    wuk_t = jnp.transpose(w_uk[layer], (1, 2, 0)).astype(BF16)
    wuv_t = jnp.transpose(w_uv[layer], (1, 0, 2)).astype(BF16)
    w_o = w_out[layer].astype(BF16)
    qg, kvg, lng, lnb = q_norm_g[layer][None], kv_norm_g[layer][None], ln_g[layer][None], ln_b[layer][None]

    n_c = BATCH + DEC_BATCH
    pad = (-n_c) % 8
    c_all = jnp.concatenate([c_prompt, c_sample, jnp.zeros((pad, D_MODEL), F32)], axis=0)
    mod = _ada_mod(c_all, w_ada[layer], b_ada[layer][None])
    mod_p = mod[:BATCH].reshape(BATCH, 1, 3 * D_MODEL)
    mod_s = jnp.repeat(mod[BATCH:n_c], DEC_SEQ, axis=0)

    tabs_p = _rope_tables_128(jnp.arange(SEQ, dtype=jnp.int32))
    tm = 256
    tabs_s = tuple(jnp.tile(t, (tm // DEC_SEQ, 1))
                   for t in _rope_tables_128(PAST_LEN + jnp.arange(DEC_SEQ, dtype=jnp.int32)))

    xp = x_prompt.reshape(rows_p, D_MODEL)
    xs = x_sample.reshape(rows_s, D_MODEL)
    proj_w = (w_p, wuq_n, wuq_r, wuk_t, qg, kvg)
    qa_p, ka_p, va_p, ga_p, ql_p, qr_p, ckv_p, kr_p, gb_p = _in_proj(
        xp, mod_p, *proj_w, tabs_p, per_row_mod=False, rows_per_batch=SEQ)
    qa_s, ka_s, va_s, ga_s, ql_s, qr_s, ckv_s, kr_s, gb_s = _in_proj(
        xs, mod_s, *proj_w, tabs_s, per_row_mod=True, rows_per_batch=DEC_SEQ)

    oa_p = _moba_prompt(qa_p.reshape(BATCH, SEQ, A_WIDTH), ka_p.reshape(BATCH, SEQ, A_KV_WIDTH),
                        va_p.reshape(BATCH, SEQ, A_KV_WIDTH))
    ob_p = _mla_prompt(ql_p.reshape(BATCH, SEQ, -1), qr_p.reshape(BATCH, SEQ, -1),
                       ckv_p.reshape(BATCH, SEQ, B_KV_LORA), kr_p.reshape(BATCH, SEQ, B_ROPE), wuv_t)
    y_p = _out_proj(oa_p.reshape(rows_p, A_WIDTH), ga_p, ob_p.reshape(rows_p, B_WIDTH), gb_p, xp, mod_p,
                    w_o, lng, lnb, per_row_mod=False, rows_per_batch=SEQ)

    cache_k2 = cache_k[layer].reshape(n_pool, PAGE_ROWS, A_HEAD_DIM)
    cache_v2 = cache_v[layer].reshape(n_pool, PAGE_ROWS, A_HEAD_DIM)
    cache_krt = jnp.swapaxes(cache_kr[layer], 1, 2)
    oa_s = _moba_sample(page_table, qa_s.reshape(DEC_BATCH, DEC_SEQ, A_WIDTH),
                        ka_s.reshape(DEC_BATCH, DEC_SEQ, A_KV_WIDTH), va_s.reshape(DEC_BATCH, DEC_SEQ, A_KV_WIDTH),
                        cache_k2, cache_v2)
    ob_s = _mla_sample(page_table, ql_s.reshape(DEC_BATCH, DEC_SEQ, -1), qr_s.reshape(DEC_BATCH, DEC_SEQ, -1),
                       ckv_s.reshape(DEC_BATCH, DEC_SEQ, B_KV_LORA), kr_s.reshape(DEC_BATCH, DEC_SEQ, B_ROPE),
                       wuv_t, cache_ckv[layer], cache_krt)
    y_s = _out_proj(oa_s.reshape(rows_s, A_WIDTH), ga_s, ob_s.reshape(rows_s, B_WIDTH), gb_s, xs, mod_s,
                    w_o, lng, lnb, per_row_mod=True, rows_per_batch=DEC_SEQ)

    return (y_p.reshape(BATCH, SEQ, D_MODEL), y_s.reshape(DEC_BATCH, DEC_SEQ, D_MODEL),
            ka_p.reshape(DEPTH, BATCH, SEQ, A_KV_HEADS, A_HEAD_DIM),
            va_p.reshape(DEPTH, BATCH, SEQ, A_KV_HEADS, A_HEAD_DIM),
            ckv_p.reshape(DEPTH, BATCH, SEQ, B_KV_LORA), kr_p.reshape(DEPTH, BATCH, SEQ, B_ROPE),
            ka_s.reshape(DEPTH, DEC_BATCH, DEC_SEQ, A_KV_HEADS, A_HEAD_DIM),
            va_s.reshape(DEPTH, DEC_BATCH, DEC_SEQ, A_KV_HEADS, A_HEAD_DIM),
            ckv_s.reshape(DEPTH, DEC_BATCH, DEC_SEQ, B_KV_LORA), kr_s.reshape(DEPTH, DEC_BATCH, DEC_SEQ, B_ROPE))
```

```python
import functools

import jax
import jax.numpy as jnp
from jax import lax
from jax.experimental import pallas as pl
from jax.experimental.pallas import tpu as pltpu

F32 = jnp.float32
BF16 = jnp.bfloat16
NEG_INF = float("-inf")

D_MODEL = 2048
BATCH = 4
SEQ = 2048
DEC_BATCH = 128
DEC_SEQ = 8
PAST_LEN = 8192
PAGE_SIZE = 128
N_PAGES = PAST_LEN // PAGE_SIZE

A_HEAD_DIM = 128
A_HEADS = 8
A_KV_HEADS = 4
A_GROUP = A_HEADS // A_KV_HEADS
A_WIDTH = A_HEADS * A_HEAD_DIM
A_KV_WIDTH = A_KV_HEADS * A_HEAD_DIM
A_BLOCK = 256
A_TOPK = 3
A_SCALE = A_HEAD_DIM ** -0.5
B_NOPE = 128
B_ROPE = 64
B_VDIM = 128
B_HEADS = 8
B_WIDTH = B_HEADS * B_VDIM
B_Q_LORA = 512
B_KV_LORA = 256
B_SCALE = (B_NOPE + B_ROPE) ** -0.5
ROPE_THETA = 10000.0
RMS_EPS = 1e-6
LN_EPS = 1e-5
DEPTH = 1
ALPHA = (2 * DEPTH) ** 0.25

N_BLOCKS_PROMPT = SEQ // A_BLOCK
N_BLOCKS_PAST = PAST_LEN // A_BLOCK
PAGES_PER_BLOCK = A_BLOCK // PAGE_SIZE

C_QA = (0, 1024)
C_KA = (1024, 1536)
C_VA = (1536, 2048)
C_GA = (2048, 3072)
C_CQ = (3072, 3584)
C_CKV = (3584, 3840)
C_GB = (3840, 4864)
C_KR = (4864, 4992)
W_IN_COLS = 4992

LANES = 128
VMEM_LIMIT_BYTES = 56 * 1024 * 1024


def _params(semantics):
    return pltpu.CompilerParams(dimension_semantics=semantics, vmem_limit_bytes=VMEM_LIMIT_BYTES)


def _silu(x):
    return x * jax.nn.sigmoid(x)


def _dot(a, b):
    return jnp.dot(a, b, preferred_element_type=F32)


def _dot_nt(a, b):
    return lax.dot_general(a, b, (((1,), (1,)), ((), ())), preferred_element_type=F32)


def _resident(shape):
    nd = len(shape)
    return pl.BlockSpec(shape, lambda *_: (0,) * nd, pipeline_mode=pl.Buffered(1))


def _ada_kernel(c_ref, w_ref, b_ref, o_ref):
    a = _silu(c_ref[...]).astype(BF16)
    o_ref[...] = _dot(a, w_ref[...].astype(BF16)) + b_ref[...]


def _ada_mod(c_all, w_ada, b_ada):
    rows = c_all.shape[0]
    n = w_ada.shape[1]
    tn = 768
    return pl.pallas_call(
        _ada_kernel,
        grid=(n // tn,),
        in_specs=[pl.BlockSpec((rows, D_MODEL), lambda j: (0, 0)),
                  pl.BlockSpec((D_MODEL, tn), lambda j: (0, j)),
                  pl.BlockSpec((1, tn), lambda j: (0, j))],
        out_specs=pl.BlockSpec((rows, tn), lambda j: (0, j)),
        out_shape=jax.ShapeDtypeStruct((rows, n), F32),
        compiler_params=_params(("arbitrary",)),
        name="ada_mod",
    )(c_all, w_ada, b_ada)


def _in_proj_kernel(x_ref, shift_ref, scale_ref, w_ref, wuqn_ref, wuqr_ref, wuk_ref, qg_ref, kvg_ref,
                    cosa_ref, sina_ref, cosb_ref, sinb_ref,
                    qa_ref, ka_ref, va_ref, ga_ref, qlat_ref, qrope_ref, ckv_ref, kr_ref, gb_ref,
                    *, per_row_mod):
    x = x_ref[...]
    if per_row_mod:
        shift, scale = shift_ref[...], scale_ref[...]
    else:
        shift, scale = shift_ref[0], scale_ref[0]
    hb = (x * (1.0 + scale) + shift).astype(BF16)
    tm = x.shape[0]
    cosa, sina = cosa_ref[...], sina_ref[...]
    cosb, sinb = cosb_ref[...], sinb_ref[...]

    def proj(cols):
        return _dot(hb, w_ref[:, cols[0]:cols[1]])

    def rope_full(z, n_heads):
        outs = []
        for h in range(n_heads):
            zh = z[:, h * LANES:(h + 1) * LANES]
            outs.append(zh * cosa + pltpu.roll(zh, 64, 1) * sina)
        return jnp.concatenate(outs, axis=1)

    lane = lax.broadcasted_iota(jnp.int32, (tm, LANES), 1)
    first_half = (lane & 63) < 32

    def rope_half(zg):
        partner = jnp.where(first_half, pltpu.roll(zg, 96, 1), pltpu.roll(zg, 32, 1))
        return zg * cosb + partner * sinb

    def rms(z, g):
        return z * lax.rsqrt(jnp.mean(z * z, axis=-1, keepdims=True) + RMS_EPS) * g

    qa_ref[...] = rope_full(proj(C_QA), A_HEADS)
    ka_ref[...] = rope_full(proj(C_KA), A_KV_HEADS)
    va_ref[...] = proj(C_VA)
    ga_ref[...] = proj(C_GA)
    gb_ref[...] = proj(C_GB)
    ckv_ref[...] = rms(proj(C_CKV), kvg_ref[...])
    kr_ref[...] = rope_half(proj(C_KR))[:, :B_ROPE]

    cqn = rms(proj(C_CQ), qg_ref[...]).astype(BF16)
    qn = _dot(cqn, wuqn_ref[...])
    qr = _dot(cqn, wuqr_ref[...])
    for h in range(B_HEADS):
        qlat_ref[:, h * B_KV_LORA:(h + 1) * B_KV_LORA] = _dot(
            qn[:, h * B_NOPE:(h + 1) * B_NOPE].astype(BF16), wuk_ref[h])
    qrope_ref[...] = jnp.concatenate(
        [rope_half(qr[:, g * LANES:(g + 1) * LANES]) for g in range(B_HEADS * B_ROPE // LANES)], axis=1)


def _in_proj(x, mod, w_p, wuq_n, wuq_r, wuk_t, q_norm_g, kv_norm_g, tabs, *, per_row_mod, rows_per_batch):
    rows = x.shape[0]
    tm = 256
    steps = rows // tm
    if per_row_mod:
        shift_spec = pl.BlockSpec((tm, D_MODEL), lambda i: (i, 0))
        scale_spec = pl.BlockSpec((tm, D_MODEL), lambda i: (i, 1))
        tab_spec = pl.BlockSpec((tm, LANES), lambda i: (0, 0))
    else:
        tiles_per_batch = rows_per_batch // tm
        shift_spec = pl.BlockSpec((1, 1, D_MODEL), lambda i: (i // tiles_per_batch, 0, 0))
        scale_spec = pl.BlockSpec((1, 1, D_MODEL), lambda i: (i // tiles_per_batch, 0, 1))
        tab_spec = pl.BlockSpec((tm, LANES), lambda i: (i % tiles_per_batch, 0))
    widths = (A_WIDTH, A_KV_WIDTH, A_KV_WIDTH, A_WIDTH, B_HEADS * B_KV_LORA, B_HEADS * B_ROPE,
              B_KV_LORA, B_ROPE, B_WIDTH)
    return pl.pallas_call(
        functools.partial(_in_proj_kernel, per_row_mod=per_row_mod),
        grid=(steps,),
        in_specs=[pl.BlockSpec((tm, D_MODEL), lambda i: (i, 0)), shift_spec, scale_spec,
                  _resident(w_p.shape), _resident(wuq_n.shape), _resident(wuq_r.shape),
                  _resident(wuk_t.shape), _resident(q_norm_g.shape), _resident(kv_norm_g.shape),
                  tab_spec, tab_spec, tab_spec, tab_spec],
        out_specs=[pl.BlockSpec((tm, w), lambda i: (i, 0)) for w in widths],
        out_shape=[jax.ShapeDtypeStruct((rows, w), F32) for w in widths],
        compiler_params=_params(("arbitrary",)),
        name="in_proj_sample" if per_row_mod else "in_proj_prompt",
    )(x, mod, mod, w_p, wuq_n, wuq_r, wuk_t, q_norm_g, kv_norm_g, *tabs)


MOBA_ROW_CHUNK = 128


def _lanes2(x):
    return jnp.concatenate([x, x], axis=1)


ONES_ROWS = 16
VT_ROWS = A_HEAD_DIM + ONES_ROWS


def _moba_prompt_kernel(q_ref, k_ref, v_ref, o_ref, kb_s, vt_s, cb_s):
    k = k_ref[0]
    kb_s[...] = k.astype(BF16)
    vt_s[:A_HEAD_DIM, :] = v_ref[0].T.astype(BF16)
    vt_s[A_HEAD_DIM:, :] = jnp.ones((ONES_ROWS, SEQ), BF16)
    km = jnp.concatenate(
        [jnp.sum(k[j * A_BLOCK:(j + 1) * A_BLOCK], axis=0, keepdims=True) * (1.0 / A_BLOCK)
         for j in range(N_BLOCKS_PROMPT)], axis=0).astype(BF16)

    rows = A_GROUP * A_BLOCK
    sub = 8
    tiles = A_BLOCK // sub
    key = lax.broadcasted_iota(jnp.int32, (A_BLOCK, rows), 0)
    tok = lax.broadcasted_iota(jnp.int32, (A_BLOCK, rows), 1) & (A_BLOCK - 1)
    cb_s[...] = jnp.where(key <= tok, 0.0, NEG_INF)
    blk = lax.broadcasted_iota(jnp.int32, (N_BLOCKS_PROMPT, rows), 0)

    for i in range(N_BLOCKS_PROMPT):
        q2 = q_ref[0, i * A_BLOCK:(i + 1) * A_BLOCK, :]
        qb = jnp.concatenate([q2[:, g * A_HEAD_DIM:(g + 1) * A_HEAD_DIM] for g in range(A_GROUP)],
                             axis=0).astype(BF16)
        n_keys = (i + 1) * A_BLOCK
        s = _dot_nt(kb_s[:n_keys, :], qb) * A_SCALE
        s3 = s.reshape(n_keys // sub, sub, rows)
        pieces = []
        if i > 0:
            sg = _dot_nt(km, qb)
            rank = jnp.zeros(sg.shape, jnp.int32)
            for jp in range(i):
                row = sg[jp:jp + 1, :]
                beats = (row > sg) | ((row == sg) & (blk > jp))
                rank = rank + jnp.where(beats, 1, 0)
            gate_bias = jnp.where(rank < A_TOPK, 0.0, NEG_INF)
            for j in range(i):
                bias_j = jnp.broadcast_to(gate_bias[j:j + 1, :], (sub, rows))
                pieces.append(s3[j * tiles:(j + 1) * tiles] + bias_j[None])
        pieces.append(s3[i * tiles:] + cb_s[...].reshape(tiles, sub, rows))
        s3 = jnp.concatenate(pieces, axis=0)
        m = jnp.max(s3, axis=0)
        for shift in (4, 2, 1):
            m = jnp.maximum(m, pltpu.roll(m, shift, 0))
        p = jnp.exp(s3 - m[None]).reshape(n_keys, rows).astype(BF16)
        acc = _dot(vt_s[:, :n_keys], p)
        denom = acc[A_HEAD_DIM:A_HEAD_DIM + sub]
        out_t = (acc[:A_HEAD_DIM].reshape(A_HEAD_DIM // sub, sub, rows) / denom[None]).reshape(A_HEAD_DIM, rows)
        o_ref[0, i * A_BLOCK:(i + 1) * A_BLOCK, :] = jnp.concatenate(
            [out_t[:, g * A_BLOCK:(g + 1) * A_BLOCK].T for g in range(A_GROUP)], axis=1)


def _moba_prompt(qa, ka, va):
    gw = A_GROUP * A_HEAD_DIM
    return pl.pallas_call(
        _moba_prompt_kernel,
        grid=(BATCH, A_KV_HEADS),
        in_specs=[pl.BlockSpec((1, SEQ, gw), lambda b, g: (b, 0, g)),
                  pl.BlockSpec((1, SEQ, A_HEAD_DIM), lambda b, g: (b, 0, g)),
                  pl.BlockSpec((1, SEQ, A_HEAD_DIM), lambda b, g: (b, 0, g))],
        out_specs=pl.BlockSpec((1, SEQ, gw), lambda b, g: (b, 0, g)),
        out_shape=jax.ShapeDtypeStruct((BATCH, SEQ, A_WIDTH), F32),
        scratch_shapes=[pltpu.VMEM((SEQ, A_HEAD_DIM), BF16), pltpu.VMEM((VT_ROWS, SEQ), BF16),
                        pltpu.VMEM((A_BLOCK, A_GROUP * A_BLOCK), F32)],
        compiler_params=_params(("arbitrary", "arbitrary")),
        name="moba_prompt",
    )(qa, ka, va)


MLA_TQ = 128
MLA_Q_BLOCKS = SEQ // MLA_TQ
CT_ROWS = B_KV_LORA + ONES_ROWS


def _mla_prompt_kernel(ql_ref, qr_ref, ckv_ref, kr_ref, wuv_ref, o_ref, ckvb_s, krb_s, ct_s, cb_s):
    i = pl.program_id(1)
    tq = MLA_TQ
    rows = B_HEADS * tq
    sub = 8

    @pl.when(i == 0)
    def _():
        ckv = ckv_ref[0]
        ckvb_s[...] = ckv.astype(BF16)
        krb_s[...] = kr_ref[0].astype(BF16)
        ct_s[:B_KV_LORA, :] = ckv.T.astype(BF16)
        ct_s[B_KV_LORA:, :] = jnp.ones((ONES_ROWS, SEQ), BF16)
        key = lax.broadcasted_iota(jnp.int32, (tq, rows), 0)
        tok = lax.broadcasted_iota(jnp.int32, (tq, rows), 1) & (tq - 1)
        cb_s[...] = jnp.where(key <= tok, 0.0, NEG_INF)

    ql = ql_ref[0]
    qr = qr_ref[0]
    qls = _stack_heads(ql, B_HEADS, B_KV_LORA).astype(BF16)
    qrs = _stack_heads(qr, B_HEADS, B_ROPE).astype(BF16)

    def block(c):
        n_keys = (c + 1) * tq
        s = (_dot_nt(ckvb_s[:n_keys, :], qls) + _dot_nt(krb_s[:n_keys, :], qrs)) * B_SCALE
        s3 = s.reshape(n_keys // sub, sub, rows)
        own = s3[c * (tq // sub):] + cb_s[...].reshape(tq // sub, sub, rows)
        s3 = own if c == 0 else jnp.concatenate([s3[:c * (tq // sub)], own], axis=0)
        m = jnp.max(s3, axis=0)
        for shift in (4, 2, 1):
            m = jnp.maximum(m, pltpu.roll(m, shift, 0))
        p = jnp.exp(s3 - m[None]).reshape(n_keys, rows).astype(BF16)
        acc = _dot(ct_s[:, :n_keys], p)
        denom = acc[B_KV_LORA:B_KV_LORA + sub]
        out_t = (acc[:B_KV_LORA].reshape(B_KV_LORA // sub, sub, rows) / denom[None]
                 ).reshape(B_KV_LORA, rows).astype(BF16)
        o_ref[0] = jnp.concatenate(
            [_dot(wuv_ref[h], out_t[:, h * tq:(h + 1) * tq]).T for h in range(B_HEADS)], axis=1)

    for c in range(MLA_Q_BLOCKS):
        @pl.when(i == c)
        def _(c=c):
            block(c)


def _mla_prompt(q_lat, q_rope, ckv, kr, wuv_vt):
    tq = MLA_TQ
    return pl.pallas_call(
        _mla_prompt_kernel,
        grid=(BATCH, MLA_Q_BLOCKS),
        in_specs=[pl.BlockSpec((1, tq, B_HEADS * B_KV_LORA), lambda b, i: (b, i, 0)),
                  pl.BlockSpec((1, tq, B_HEADS * B_ROPE), lambda b, i: (b, i, 0)),
                  pl.BlockSpec((1, SEQ, B_KV_LORA), lambda b, i: (b, 0, 0)),
                  pl.BlockSpec((1, SEQ, B_ROPE), lambda b, i: (b, 0, 0)),
                  _resident(wuv_vt.shape)],
        out_specs=pl.BlockSpec((1, tq, B_WIDTH), lambda b, i: (b, i, 0)),
        out_shape=jax.ShapeDtypeStruct((BATCH, SEQ, B_WIDTH), F32),
        scratch_shapes=[pltpu.VMEM((SEQ, B_KV_LORA), BF16), pltpu.VMEM((SEQ, B_ROPE), BF16),
                        pltpu.VMEM((CT_ROWS, SEQ), BF16), pltpu.VMEM((tq, B_HEADS * tq), F32)],
        compiler_params=_params(("arbitrary", "arbitrary")),
        name="mla_prompt",
    )(q_lat, q_rope, ckv, kr, wuv_vt)


SAMPLE_ROWS = A_HEADS * DEC_SEQ
KV_ROWS = A_GROUP * DEC_SEQ


def _stack_heads(x, n_heads, width):
    return jnp.concatenate([x[:, h * width:(h + 1) * width] for h in range(n_heads)], axis=0)


def _unstack_heads(x, n_heads, tokens):
    return jnp.concatenate([x[h * tokens:(h + 1) * tokens] for h in range(n_heads)], axis=1)


def _causal_new_tokens(shape):
    tq = lax.broadcasted_iota(jnp.int32, shape, 0) & (DEC_SEQ - 1)
    tk = lax.broadcasted_iota(jnp.int32, shape, 1)
    return tk <= tq


def _log2(n):
    assert n & (n - 1) == 0
    return n.bit_length() - 1


def _set_lane(acc, lane, idx, col):
    return jnp.where(lane == idx, col, acc)


MOBA_GROUP = 4
MOBA_GROUPS = N_BLOCKS_PAST // MOBA_GROUP
PAGE_ROWS = PAGE_SIZE * A_KV_HEADS
BLOCK_ROWS = A_BLOCK * A_KV_HEADS
RING_SLOTS = 4
RING_LOOKAHEAD = RING_SLOTS - 1


def _moba_sample_kernel(pt_ref, q_ref, kn_ref, vn_ref, ck_ref, cv_ref, o_ref, kbuf, vbuf, sems, opart_s):
    n = pl.program_id(0)
    total = pl.num_programs(0) * MOBA_GROUPS

    def group_copies(seq, grp, half):
        copies = []
        for u in range(MOBA_GROUP * PAGES_PER_BLOCK):
            page = pt_ref[seq, MOBA_GROUP * PAGES_PER_BLOCK * grp + u]
            rows = pl.ds(u * PAGE_ROWS, PAGE_ROWS)
            copies.append((pltpu.make_async_copy(ck_ref.at[page], kbuf.at[half, rows], sems.at[0, half]), 0))
            copies.append((pltpu.make_async_copy(cv_ref.at[page], vbuf.at[half, rows], sems.at[1, half]), 1))
        return copies

    @pl.when(n == 0)
    def _():
        for g0 in range(RING_LOOKAHEAD):
            for cp, prio in group_copies(0, g0, g0 % RING_SLOTS):
                cp.start(priority=prio)

    qf = _stack_heads(q_ref[0], A_HEADS, A_HEAD_DIM)
    qb = qf.astype(BF16)
    row_head = lax.shift_right_logical(lax.broadcasted_iota(jnp.int32, (SAMPLE_ROWS, BLOCK_ROWS), 0),
                                       _log2(KV_ROWS))
    col_head = lax.broadcasted_iota(jnp.int32, (SAMPLE_ROWS, BLOCK_ROWS), 1) & (A_KV_HEADS - 1)
    same_head = row_head == col_head
    lane = lax.broadcasted_iota(jnp.int32, (SAMPLE_ROWS, LANES), 1)

    def group_partials(j0, half, stats):
        m_all, l_all, sg_all = stats
        block_rows = [slice(b * BLOCK_ROWS, (b + 1) * BLOCK_ROWS) for b in range(MOBA_GROUP)]
        scores, gates = [], []
        for rows in block_rows:
            kf = kbuf[half, rows, :]
            ks = jnp.sum(kf.reshape(BLOCK_ROWS // 8, 8, A_HEAD_DIM), axis=0)
            km = (ks[:A_KV_HEADS] + ks[A_KV_HEADS:]) * (1.0 / A_BLOCK)
            km_rows = jnp.concatenate(
                [jnp.broadcast_to(km[g:g + 1], (KV_ROWS, A_HEAD_DIM)) for g in range(A_KV_HEADS)], axis=0)
            gates.append(jnp.sum(qf * km_rows, axis=1, keepdims=True))
            scores.append(_dot_nt(qb, kf.astype(BF16)))
        probs = []
        for b in range(MOBA_GROUP):
            s = jnp.where(same_head, scores[b] * A_SCALE, NEG_INF)
            m = jnp.max(s, axis=1, keepdims=True)
            p = jnp.exp(s - m)
            probs.append(p.astype(BF16))
            m_all = _set_lane(m_all, lane, j0 + b, m)
            l_all = _set_lane(l_all, lane, j0 + b, jnp.sum(p, axis=1, keepdims=True))
            sg_all = _set_lane(sg_all, lane, j0 + b, gates[b])
        for b, rows in enumerate(block_rows):
            opart_s[j0 + b] = _dot(probs[b], vbuf[half, rows, :].astype(BF16))
        return m_all, l_all, sg_all

    def ring_step(jj, stats):
        for slot in range(RING_SLOTS):
            grp = jj * RING_SLOTS + slot
            nxt = n * MOBA_GROUPS + grp + RING_LOOKAHEAD

            @pl.when(nxt < total)
            def _():
                for cp, prio in group_copies(lax.shift_right_logical(nxt, _log2(MOBA_GROUPS)),
                                             nxt & (MOBA_GROUPS - 1), (slot + RING_LOOKAHEAD) % RING_SLOTS):
                    cp.start(priority=prio)

            for cp, _ in group_copies(n, grp, slot):
                cp.wait()
            stats = group_partials(grp * MOBA_GROUP, slot, stats)
        return stats

    zeros = jnp.zeros((SAMPLE_ROWS, LANES), F32)
    mm, ll, sg = lax.fori_loop(0, MOBA_GROUPS // RING_SLOTS, ring_step, (zeros, zeros, zeros))

    rank = jnp.zeros(sg.shape, jnp.int32)
    for jp in range(N_BLOCKS_PAST):
        col = sg[:, jp:jp + 1]
        beats = (col > sg) | ((col == sg) & (lane > jp))
        rank = rank + jnp.where(beats, 1, 0)
    sel = (rank < A_TOPK) & (lane < N_BLOCKS_PAST)

    kn, vn = kn_ref[0], vn_ref[0]
    mo, lo, oo = [], [], []
    for g in range(A_KV_HEADS):
        rs = slice(g * KV_ROWS, (g + 1) * KV_ROWS)
        cs = slice(g * A_HEAD_DIM, (g + 1) * A_HEAD_DIM)
        s = _dot_nt(qb[rs], kn[:, cs].astype(BF16)) * A_SCALE
        s = jnp.where(_causal_new_tokens(s.shape), s, NEG_INF)
        m = jnp.max(s, axis=1, keepdims=True)
        p = jnp.exp(s - m)
        mo.append(m)
        lo.append(jnp.sum(p, axis=1, keepdims=True))
        oo.append(_dot(p.astype(BF16), vn[:, cs].astype(BF16)))
    m_own = jnp.concatenate(mo, axis=0)
    l_own = jnp.concatenate(lo, axis=0)
    o_own = jnp.concatenate(oo, axis=0)

    m_all = jnp.maximum(m_own, jnp.max(jnp.where(sel, mm, NEG_INF), axis=1, keepdims=True))
    w = jnp.where(sel, jnp.exp(mm - m_all), 0.0)
    w_own = jnp.exp(m_own - m_all)
    denom = jnp.sum(w * ll, axis=1, keepdims=True) + w_own * l_own
    acc = w_own * o_own
    for jb in range(N_BLOCKS_PAST):
        acc = acc + w[:, jb:jb + 1] * opart_s[jb]
    o_ref[0] = _unstack_heads(acc / denom, A_HEADS, DEC_SEQ)


def _moba_sample(page_table, qa, ka_new, va_new, cache_k2, cache_v2):
    assert MOBA_GROUPS % RING_SLOTS == 0

    def seq_spec(width):
        return pl.BlockSpec((1, DEC_SEQ, width), lambda n, pt: (n, 0, 0))

    grid_spec = pltpu.PrefetchScalarGridSpec(
        num_scalar_prefetch=1,
        grid=(DEC_BATCH,),
        in_specs=[seq_spec(A_WIDTH), seq_spec(A_KV_WIDTH), seq_spec(A_KV_WIDTH),
                  pl.BlockSpec(memory_space=pl.ANY), pl.BlockSpec(memory_space=pl.ANY)],
        out_specs=seq_spec(A_WIDTH),
        scratch_shapes=[pltpu.VMEM((RING_SLOTS, MOBA_GROUP * BLOCK_ROWS, A_HEAD_DIM), F32),
                        pltpu.VMEM((RING_SLOTS, MOBA_GROUP * BLOCK_ROWS, A_HEAD_DIM), F32),
                        pltpu.SemaphoreType.DMA((2, RING_SLOTS)),
                        pltpu.VMEM((N_BLOCKS_PAST, SAMPLE_ROWS, A_HEAD_DIM), F32)])
    return pl.pallas_call(
        _moba_sample_kernel,
        grid_spec=grid_spec,
        out_shape=jax.ShapeDtypeStruct((DEC_BATCH, DEC_SEQ, A_WIDTH), F32),
        compiler_params=_params(("arbitrary",)),
        name="moba_sample",
    )(page_table, qa, ka_new, va_new, cache_k2, cache_v2)


MLA_GROUP_PAGES = 16
MLA_GROUPS = N_PAGES // MLA_GROUP_PAGES
MLA_CHUNK_PAGES = 4
MLA_CHUNK_KEYS = MLA_CHUNK_PAGES * PAGE_SIZE
MLA_GROUP_CHUNKS = MLA_GROUP_PAGES // MLA_CHUNK_PAGES
MLA_CHUNKS = N_PAGES // MLA_CHUNK_PAGES
MLA_SAMPLE_ROWS = B_HEADS * DEC_SEQ


def _mla_sample_kernel(pt_ref, ql_ref, qr_ref, cn_ref, rn_ref, wuv_ref, cc_ref, cr_ref, o_ref,
                       cbuf, rbuf, sems, opart_s):
    n = pl.program_id(0)
    total = pl.num_programs(0) * MLA_GROUPS

    def group_copies(seq, grp, half):
        copies = []
        for u in range(MLA_GROUP_PAGES):
            page = pt_ref[seq, MLA_GROUP_PAGES * grp + u]
            copies.append((pltpu.make_async_copy(cc_ref.at[page], cbuf.at[half, pl.ds(u * PAGE_SIZE, PAGE_SIZE)],
                                                 sems.at[0, half]), u % 2))
            copies.append((pltpu.make_async_copy(cr_ref.at[page], rbuf.at[half, u], sems.at[1, half]), (u + 1) % 2))
        return copies

    @pl.when(n == 0)
    def _():
        for g0 in range(RING_LOOKAHEAD):
            for cp, prio in group_copies(0, g0, g0 % RING_SLOTS):
                cp.start(priority=prio)

    qls = _stack_heads(ql_ref[0], B_HEADS, B_KV_LORA).astype(BF16)
    qrs = _stack_heads(qr_ref[0], B_HEADS, B_ROPE).astype(BF16)
    lane = lax.broadcasted_iota(jnp.int32, (MLA_SAMPLE_ROWS, LANES), 1)

    def group_partials(c0, half, stats):
        m_all, l_all = stats
        latents, scores = [], []
        for k in range(MLA_GROUP_CHUNKS):
            cb = cbuf[half, k * MLA_CHUNK_KEYS:(k + 1) * MLA_CHUNK_KEYS, :].astype(BF16)
            rt = jnp.concatenate([rbuf[half, k * MLA_CHUNK_PAGES + u] for u in range(MLA_CHUNK_PAGES)],
                                 axis=1).astype(BF16)
            latents.append(cb)
            scores.append(_dot_nt(qls, cb) + _dot(qrs, rt))
        probs = []
        for k in range(MLA_GROUP_CHUNKS):
            s = scores[k] * B_SCALE
            m = jnp.max(s, axis=1, keepdims=True)
            p = jnp.exp(s - m)
            probs.append(p.astype(BF16))
            m_all = _set_lane(m_all, lane, c0 + k, m)
            l_all = _set_lane(l_all, lane, c0 + k, jnp.sum(p, axis=1, keepdims=True))
        for k in range(MLA_GROUP_CHUNKS):
            opart_s[c0 + k] = _dot(probs[k], latents[k])
        return m_all, l_all

    def ring_step(jj, stats):
        for slot in range(RING_SLOTS):
            grp = jj * RING_SLOTS + slot
            nxt = n * MLA_GROUPS + grp + RING_LOOKAHEAD

            @pl.when(nxt < total)
            def _():
                for cp, prio in group_copies(lax.shift_right_logical(nxt, _log2(MLA_GROUPS)),
                                             nxt & (MLA_GROUPS - 1), (slot + RING_LOOKAHEAD) % RING_SLOTS):
                    cp.start(priority=prio)

            for cp, _ in group_copies(n, grp, slot):
                cp.wait()
            stats = group_partials(grp * MLA_GROUP_CHUNKS, slot, stats)
        return stats

    zeros = jnp.zeros((MLA_SAMPLE_ROWS, LANES), F32)
    mm, ll = lax.fori_loop(0, MLA_GROUPS // RING_SLOTS, ring_step, (zeros, zeros))

    cn = cn_ref[0].astype(BF16)
    sn = (_dot_nt(qls, cn) + _dot_nt(qrs, rn_ref[0].astype(BF16))) * B_SCALE
    sn = jnp.where(_causal_new_tokens(sn.shape), sn, NEG_INF)
    m_own = jnp.max(sn, axis=1, keepdims=True)
    p_own = jnp.exp(sn - m_own)
    l_own = jnp.sum(p_own, axis=1, keepdims=True)
    o_own = _dot(p_own.astype(BF16), cn)

    cached = lane < MLA_CHUNKS
    m_all = jnp.maximum(m_own, jnp.max(jnp.where(cached, mm, NEG_INF), axis=1, keepdims=True))
    w = jnp.where(cached, jnp.exp(mm - m_all), 0.0)
    w_own = jnp.exp(m_own - m_all)
    denom = jnp.sum(w * ll, axis=1, keepdims=True) + w_own * l_own
    acc = w_own * o_own
    for c in range(MLA_CHUNKS):
        acc = acc + w[:, c:c + 1] * opart_s[c]
    out = (acc / denom).astype(BF16)
    o_ref[0] = jnp.concatenate(
        [_dot(out[h * DEC_SEQ:(h + 1) * DEC_SEQ], wuv_ref[h]) for h in range(B_HEADS)], axis=1)


def _mla_sample(page_table, q_lat, q_rope, ckv_new, kr_new, wuv_t, cache_ckv2, cache_krt):
    assert MLA_GROUPS % RING_SLOTS == 0 and MLA_CHUNKS <= LANES

    def seq_spec(width):
        return pl.BlockSpec((1, DEC_SEQ, width), lambda n, pt: (n, 0, 0))

    grid_spec = pltpu.PrefetchScalarGridSpec(
        num_scalar_prefetch=1,
        grid=(DEC_BATCH,),
        in_specs=[seq_spec(B_HEADS * B_KV_LORA), seq_spec(B_HEADS * B_ROPE), seq_spec(B_KV_LORA),
                  seq_spec(B_ROPE),
                  pl.BlockSpec(wuv_t.shape, lambda n, pt: (0, 0, 0), pipeline_mode=pl.Buffered(1)),
                  pl.BlockSpec(memory_space=pl.ANY), pl.BlockSpec(memory_space=pl.ANY)],
        out_specs=seq_spec(B_WIDTH),
        scratch_shapes=[pltpu.VMEM((RING_SLOTS, MLA_GROUP_PAGES * PAGE_SIZE, B_KV_LORA), F32),
                        pltpu.VMEM((RING_SLOTS, MLA_GROUP_PAGES, B_ROPE, PAGE_SIZE), F32),
                        pltpu.SemaphoreType.DMA((2, RING_SLOTS)),
                        pltpu.VMEM((MLA_CHUNKS, MLA_SAMPLE_ROWS, B_KV_LORA), F32)])
    return pl.pallas_call(
        _mla_sample_kernel,
        grid_spec=grid_spec,
        out_shape=jax.ShapeDtypeStruct((DEC_BATCH, DEC_SEQ, B_WIDTH), F32),
        compiler_params=_params(("arbitrary",)),
        name="mla_sample",
    )(page_table, q_lat, q_rope, ckv_new, kr_new, wuv_t, cache_ckv2, cache_krt)


def _out_proj_kernel(oa_ref, ga_ref, ob_ref, gb_ref, x_ref, gate_ref, w_ref, lng_ref, lnb_ref, y_ref,
                     *, per_row_mod):
    u = jnp.concatenate([oa_ref[...] * _silu(ga_ref[...]), ob_ref[...] * _silu(gb_ref[...])], axis=1)
    o = _dot(u.astype(BF16), w_ref[...])
    gate = gate_ref[...] if per_row_mod else gate_ref[0]
    r = ALPHA * x_ref[...] + gate * o
    mu = jnp.mean(r, axis=-1, keepdims=True)
    d = r - mu
    var = jnp.mean(d * d, axis=-1, keepdims=True)
    y_ref[...] = d * lax.rsqrt(var + LN_EPS) * lng_ref[...] + lnb_ref[...]


def _out_proj(oa, ga, ob, gb, x, mod, w_out, ln_g, ln_b, *, per_row_mod, rows_per_batch):
    rows = x.shape[0]
    tm = 256
    if per_row_mod:
        gate_spec = pl.BlockSpec((tm, D_MODEL), lambda i: (i, 2))
    else:
        tiles_per_batch = rows_per_batch // tm
        gate_spec = pl.BlockSpec((1, 1, D_MODEL), lambda i: (i // tiles_per_batch, 0, 2))
    half = pl.BlockSpec((tm, A_WIDTH), lambda i: (i, 0))
    full = pl.BlockSpec((tm, D_MODEL), lambda i: (i, 0))
    return pl.pallas_call(
        functools.partial(_out_proj_kernel, per_row_mod=per_row_mod),
        grid=(rows // tm,),
        in_specs=[half, half, half, half, full, gate_spec,
                  _resident(w_out.shape), _resident(ln_g.shape), _resident(ln_b.shape)],
        out_specs=full,
        out_shape=jax.ShapeDtypeStruct((rows, D_MODEL), F32),
        compiler_params=_params(("arbitrary",)),
        name="out_proj_sample" if per_row_mod else "out_proj_prompt",
    )(oa, ga, ob, gb, x, mod, w_out, ln_g, ln_b)


def _rope_tables(pos, half):
    inv = jnp.power(ROPE_THETA, -jnp.arange(half, dtype=F32) / half)
    ang = pos.astype(F32)[:, None] * inv[None, :]
    cos, sin = jnp.cos(ang), jnp.sin(ang)
    return jnp.concatenate([cos, cos], axis=1), jnp.concatenate([-sin, sin], axis=1)


def _rope_tables_128(pos):
    cosa, sina = _rope_tables(pos, A_HEAD_DIM // 2)
    cosb, sinb = _rope_tables(pos, B_ROPE // 2)
    return cosa, sina, jnp.tile(cosb, (1, 2)), jnp.tile(sinb, (1, 2))


def kernel(x_prompt, x_sample, cache_k, cache_v, cache_ckv, cache_kr, page_table, c_prompt, c_sample,
           w_ada, b_ada, w_in, q_norm_g, w_uq, kv_norm_g, w_uk, w_uv, w_out, ln_g, ln_b):
    layer = 0
    n_pool = cache_k.shape[1]
    rows_p = BATCH * SEQ
    rows_s = DEC_BATCH * DEC_SEQ

    w = w_in[layer]
    w_p = jnp.concatenate([w[:, :3840], w[:, 3904:], w[:, 3840:3904], jnp.zeros((D_MODEL, 64), F32)],
                          axis=1).astype(BF16)
    wuq = w_uq[layer].reshape(B_Q_LORA, B_HEADS, B_NOPE + B_ROPE)
    wuq_n = wuq[:, :, :B_NOPE].reshape(B_Q_LORA, B_HEADS * B_NOPE).astype(BF16)
    wuq_r = wuq[:, :, B_NOPE:].reshape(B_Q_LORA, B_HEADS * B_ROPE).astype(BF16)
    wuk_t = jnp.transpose(w_uk[layer], (1, 2, 0)).astype(BF16)
    wuv_t = jnp.transpose(w_uv[layer], (1, 0, 2)).astype(BF16)
    wuv_vt = jnp.transpose(w_uv[layer], (1, 2, 0)).astype(BF16)
    w_o = w_out[layer].astype(BF16)
    qg, kvg, lng, lnb = q_norm_g[layer][None], kv_norm_g[layer][None], ln_g[layer][None], ln_b[layer][None]

    n_c = BATCH + DEC_BATCH
    pad = (-n_c) % 8
    c_all = jnp.concatenate([c_prompt, c_sample, jnp.zeros((pad, D_MODEL), F32)], axis=0)
    mod = _ada_mod(c_all, w_ada[layer], b_ada[layer][None])
    mod_p = mod[:BATCH].reshape(BATCH, 1, 3 * D_MODEL)
    mod_s = jnp.repeat(mod[BATCH:n_c], DEC_SEQ, axis=0)

    tabs_p = _rope_tables_128(jnp.arange(SEQ, dtype=jnp.int32))
    tm = 256
    tabs_s = tuple(jnp.tile(t, (tm // DEC_SEQ, 1))
                   for t in _rope_tables_128(PAST_LEN + jnp.arange(DEC_SEQ, dtype=jnp.int32)))

    xp = x_prompt.reshape(rows_p, D_MODEL)
    xs = x_sample.reshape(rows_s, D_MODEL)
    proj_w = (w_p, wuq_n, wuq_r, wuk_t, qg, kvg)
    qa_p, ka_p, va_p, ga_p, ql_p, qr_p, ckv_p, kr_p, gb_p = _in_proj(
        xp, mod_p, *proj_w, tabs_p, per_row_mod=False, rows_per_batch=SEQ)
    qa_s, ka_s, va_s, ga_s, ql_s, qr_s, ckv_s, kr_s, gb_s = _in_proj(
        xs, mod_s, *proj_w, tabs_s, per_row_mod=True, rows_per_batch=DEC_SEQ)

    oa_p = _moba_prompt(qa_p.reshape(BATCH, SEQ, A_WIDTH), ka_p.reshape(BATCH, SEQ, A_KV_WIDTH),
                        va_p.reshape(BATCH, SEQ, A_KV_WIDTH))
    ob_p = _mla_prompt(ql_p.reshape(BATCH, SEQ, -1), qr_p.reshape(BATCH, SEQ, -1),
                       ckv_p.reshape(BATCH, SEQ, B_KV_LORA), kr_p.reshape(BATCH, SEQ, B_ROPE), wuv_vt)
    y_p = _out_proj(oa_p.reshape(rows_p, A_WIDTH), ga_p, ob_p.reshape(rows_p, B_WIDTH), gb_p, xp, mod_p,
                    w_o, lng, lnb, per_row_mod=False, rows_per_batch=SEQ)

    cache_k2 = cache_k[layer].reshape(n_pool, PAGE_ROWS, A_HEAD_DIM)
    cache_v2 = cache_v[layer].reshape(n_pool, PAGE_ROWS, A_HEAD_DIM)
    cache_krt = jnp.swapaxes(cache_kr[layer], 1, 2)
    oa_s = _moba_sample(page_table, qa_s.reshape(DEC_BATCH, DEC_SEQ, A_WIDTH),
                        ka_s.reshape(DEC_BATCH, DEC_SEQ, A_KV_WIDTH), va_s.reshape(DEC_BATCH, DEC_SEQ, A_KV_WIDTH),
                        cache_k2, cache_v2)
    ob_s = _mla_sample(page_table, ql_s.reshape(DEC_BATCH, DEC_SEQ, -1), qr_s.reshape(DEC_BATCH, DEC_SEQ, -1),
                       ckv_s.reshape(DEC_BATCH, DEC_SEQ, B_KV_LORA), kr_s.reshape(DEC_BATCH, DEC_SEQ, B_ROPE),
                       wuv_t, cache_ckv[layer], cache_krt)
    y_s = _out_proj(oa_s.reshape(rows_s, A_WIDTH), ga_s, ob_s.reshape(rows_s, B_WIDTH), gb_s, xs, mod_s,
                    w_o, lng, lnb, per_row_mod=True, rows_per_batch=DEC_SEQ)

    return (y_p.reshape(BATCH, SEQ, D_MODEL), y_s.reshape(DEC_BATCH, DEC_SEQ, D_MODEL),
            ka_p.reshape(DEPTH, BATCH, SEQ, A_KV_HEADS, A_HEAD_DIM),
            va_p.reshape(DEPTH, BATCH, SEQ, A_KV_HEADS, A_HEAD_DIM),
            ckv_p.reshape(DEPTH, BATCH, SEQ, B_KV_LORA), kr_p.reshape(DEPTH, BATCH, SEQ, B_ROPE),
            ka_s.reshape(DEPTH, DEC_BATCH, DEC_SEQ, A_KV_HEADS, A_HEAD_DIM),
            va_s.reshape(DEPTH, DEC_BATCH, DEC_SEQ, A_KV_HEADS, A_HEAD_DIM),
            ckv_s.reshape(DEPTH, DEC_BATCH, DEC_SEQ, B_KV_LORA), kr_s.reshape(DEPTH, DEC_BATCH, DEC_SEQ, B_ROPE))
```

```python
import functools

import jax
import jax.numpy as jnp
from jax import lax
from jax.experimental import pallas as pl
from jax.experimental.pallas import tpu as pltpu

F32 = jnp.float32
BF16 = jnp.bfloat16
NEG_INF = float("-inf")

D_MODEL = 2048
BATCH = 4
SEQ = 2048
DEC_BATCH = 128
DEC_SEQ = 8
PAST_LEN = 8192
PAGE_SIZE = 128
N_PAGES = PAST_LEN // PAGE_SIZE

A_HEAD_DIM = 128
A_HEADS = 8
A_KV_HEADS = 4
A_GROUP = A_HEADS // A_KV_HEADS
A_WIDTH = A_HEADS * A_HEAD_DIM
A_KV_WIDTH = A_KV_HEADS * A_HEAD_DIM
A_BLOCK = 256
A_TOPK = 3
A_SCALE = A_HEAD_DIM ** -0.5
B_NOPE = 128
B_ROPE = 64
B_VDIM = 128
B_HEADS = 8
B_WIDTH = B_HEADS * B_VDIM
B_Q_LORA = 512
B_KV_LORA = 256
B_SCALE = (B_NOPE + B_ROPE) ** -0.5
ROPE_THETA = 10000.0
RMS_EPS = 1e-6
LN_EPS = 1e-5
DEPTH = 1
ALPHA = (2 * DEPTH) ** 0.25

N_BLOCKS_PROMPT = SEQ // A_BLOCK
N_BLOCKS_PAST = PAST_LEN // A_BLOCK
PAGES_PER_BLOCK = A_BLOCK // PAGE_SIZE

C_QA = (0, 1024)
C_KA = (1024, 1536)
C_VA = (1536, 2048)
C_GA = (2048, 3072)
C_CQ = (3072, 3584)
C_CKV = (3584, 3840)
C_GB = (3840, 4864)
C_KR = (4864, 4992)
W_IN_COLS = 4992

LANES = 128
VMEM_LIMIT_BYTES = 56 * 1024 * 1024


def _params(semantics):
    return pltpu.CompilerParams(dimension_semantics=semantics, vmem_limit_bytes=VMEM_LIMIT_BYTES)


def _silu(x):
    return x * jax.nn.sigmoid(x)


def _dot(a, b):
    return jnp.dot(a, b, preferred_element_type=F32)


def _dot_nt(a, b):
    return lax.dot_general(a, b, (((1,), (1,)), ((), ())), preferred_element_type=F32)


def _resident(shape):
    nd = len(shape)
    return pl.BlockSpec(shape, lambda *_: (0,) * nd, pipeline_mode=pl.Buffered(1))


def _ada_kernel(c_ref, w_ref, b_ref, o_ref):
    a = _silu(c_ref[...]).astype(BF16)
    o_ref[...] = _dot(a, w_ref[...].astype(BF16)) + b_ref[...]


def _ada_mod(c_all, w_ada, b_ada):
    rows = c_all.shape[0]
    n = w_ada.shape[1]
    tn = 768
    return pl.pallas_call(
        _ada_kernel,
        grid=(n // tn,),
        in_specs=[pl.BlockSpec((rows, D_MODEL), lambda j: (0, 0)),
                  pl.BlockSpec((D_MODEL, tn), lambda j: (0, j)),
                  pl.BlockSpec((1, tn), lambda j: (0, j))],
        out_specs=pl.BlockSpec((rows, tn), lambda j: (0, j)),
        out_shape=jax.ShapeDtypeStruct((rows, n), F32),
        compiler_params=_params(("arbitrary",)),
        name="ada_mod",
    )(c_all, w_ada, b_ada)


def _in_proj_kernel(x_ref, shift_ref, scale_ref, w_ref, wuqn_ref, wuqr_ref, wuk_ref, qg_ref, kvg_ref,
                    cosa_ref, sina_ref, cosb_ref, sinb_ref,
                    qa_ref, ka_ref, va_ref, ga_ref, qlat_ref, qrope_ref, ckv_ref, kr_ref, gb_ref,
                    *, per_row_mod):
    x = x_ref[...]
    if per_row_mod:
        shift, scale = shift_ref[...], scale_ref[...]
    else:
        shift, scale = shift_ref[0], scale_ref[0]
    hb = (x * (1.0 + scale) + shift).astype(BF16)
    tm = x.shape[0]
    cosa, sina = cosa_ref[...], sina_ref[...]
    cosb, sinb = cosb_ref[...], sinb_ref[...]

    def proj(cols):
        return _dot(hb, w_ref[:, cols[0]:cols[1]])

    def rope_full(z, n_heads):
        outs = []
        for h in range(n_heads):
            zh = z[:, h * LANES:(h + 1) * LANES]
            outs.append(zh * cosa + pltpu.roll(zh, 64, 1) * sina)
        return jnp.concatenate(outs, axis=1)

    lane = lax.broadcasted_iota(jnp.int32, (tm, LANES), 1)
    first_half = (lane & 63) < 32

    def rope_half(zg):
        partner = jnp.where(first_half, pltpu.roll(zg, 96, 1), pltpu.roll(zg, 32, 1))
        return zg * cosb + partner * sinb

    def rms(z, g):
        return z * lax.rsqrt(jnp.mean(z * z, axis=-1, keepdims=True) + RMS_EPS) * g

    qa_ref[...] = rope_full(proj(C_QA), A_HEADS)
    ka_ref[...] = rope_full(proj(C_KA), A_KV_HEADS)
    va_ref[...] = proj(C_VA)
    ga_ref[...] = proj(C_GA)
    gb_ref[...] = proj(C_GB)
    ckv_ref[...] = rms(proj(C_CKV), kvg_ref[...])
    kr_ref[...] = rope_half(proj(C_KR))[:, :B_ROPE]

    cqn = rms(proj(C_CQ), qg_ref[...]).astype(BF16)
    qn = _dot(cqn, wuqn_ref[...])
    qr = _dot(cqn, wuqr_ref[...])
    for h in range(B_HEADS):
        qlat_ref[:, h * B_KV_LORA:(h + 1) * B_KV_LORA] = _dot(
            qn[:, h * B_NOPE:(h + 1) * B_NOPE].astype(BF16), wuk_ref[h])
    qrope_ref[...] = jnp.concatenate(
        [rope_half(qr[:, g * LANES:(g + 1) * LANES]) for g in range(B_HEADS * B_ROPE // LANES)], axis=1)


def _in_proj(x, mod, w_p, wuq_n, wuq_r, wuk_t, q_norm_g, kv_norm_g, tabs, *, per_row_mod, rows_per_batch):
    rows = x.shape[0]
    tm = 256
    steps = rows // tm
    if per_row_mod:
        shift_spec = pl.BlockSpec((tm, D_MODEL), lambda i: (i, 0))
        scale_spec = pl.BlockSpec((tm, D_MODEL), lambda i: (i, 1))
        tab_spec = pl.BlockSpec((tm, LANES), lambda i: (0, 0))
    else:
        tiles_per_batch = rows_per_batch // tm
        shift_spec = pl.BlockSpec((1, 1, D_MODEL), lambda i: (i // tiles_per_batch, 0, 0))
        scale_spec = pl.BlockSpec((1, 1, D_MODEL), lambda i: (i // tiles_per_batch, 0, 1))
        tab_spec = pl.BlockSpec((tm, LANES), lambda i: (i % tiles_per_batch, 0))
    widths = (A_WIDTH, A_KV_WIDTH, A_KV_WIDTH, A_WIDTH, B_HEADS * B_KV_LORA, B_HEADS * B_ROPE,
              B_KV_LORA, B_ROPE, B_WIDTH)
    return pl.pallas_call(
        functools.partial(_in_proj_kernel, per_row_mod=per_row_mod),
        grid=(steps,),
        in_specs=[pl.BlockSpec((tm, D_MODEL), lambda i: (i, 0)), shift_spec, scale_spec,
                  _resident(w_p.shape), _resident(wuq_n.shape), _resident(wuq_r.shape),
                  _resident(wuk_t.shape), _resident(q_norm_g.shape), _resident(kv_norm_g.shape),
                  tab_spec, tab_spec, tab_spec, tab_spec],
        out_specs=[pl.BlockSpec((tm, w), lambda i: (i, 0)) for w in widths],
        out_shape=[jax.ShapeDtypeStruct((rows, w), F32) for w in widths],
        compiler_params=_params(("arbitrary",)),
        name="in_proj_sample" if per_row_mod else "in_proj_prompt",
    )(x, mod, mod, w_p, wuq_n, wuq_r, wuk_t, q_norm_g, kv_norm_g, *tabs)


MOBA_ROW_CHUNK = 128


def _lanes2(x):
    return jnp.concatenate([x, x], axis=1)


ONES_ROWS = 16
VT_ROWS = A_HEAD_DIM + ONES_ROWS


def _moba_prompt_kernel(q_ref, k_ref, v_ref, o_ref, kb_s, vt_s, cb_s):
    k = k_ref[0]
    kb_s[...] = k.astype(BF16)
    vt_s[:A_HEAD_DIM, :] = v_ref[0].T.astype(BF16)
    vt_s[A_HEAD_DIM:, :] = jnp.ones((ONES_ROWS, SEQ), BF16)
    km = jnp.concatenate(
        [jnp.sum(k[j * A_BLOCK:(j + 1) * A_BLOCK], axis=0, keepdims=True) * (1.0 / A_BLOCK)
         for j in range(N_BLOCKS_PROMPT)], axis=0).astype(BF16)

    rows = A_GROUP * A_BLOCK
    sub = 8
    tiles = A_BLOCK // sub
    key = lax.broadcasted_iota(jnp.int32, (A_BLOCK, rows), 0)
    tok = lax.broadcasted_iota(jnp.int32, (A_BLOCK, rows), 1) & (A_BLOCK - 1)
    cb_s[...] = jnp.where(key <= tok, 0.0, NEG_INF)
    blk = lax.broadcasted_iota(jnp.int32, (N_BLOCKS_PROMPT, rows), 0)

    for i in range(N_BLOCKS_PROMPT):
        q2 = q_ref[0, i * A_BLOCK:(i + 1) * A_BLOCK, :]
        qb = jnp.concatenate([q2[:, g * A_HEAD_DIM:(g + 1) * A_HEAD_DIM] for g in range(A_GROUP)],
                             axis=0).astype(BF16)
        n_keys = (i + 1) * A_BLOCK
        s = _dot_nt(kb_s[:n_keys, :], qb) * A_SCALE
        s3 = s.reshape(n_keys // sub, sub, rows)
        pieces = []
        if i > 0:
            sg = _dot_nt(km, qb)
            rank = jnp.zeros(sg.shape, jnp.int32)
            for jp in range(i):
                row = sg[jp:jp + 1, :]
                beats = (row > sg) | ((row == sg) & (blk > jp))
                rank = rank + jnp.where(beats, 1, 0)
            gate_bias = jnp.where(rank < A_TOPK, 0.0, NEG_INF)
            for j in range(i):
                bias_j = jnp.broadcast_to(gate_bias[j:j + 1, :], (sub, rows))
                pieces.append(s3[j * tiles:(j + 1) * tiles] + bias_j[None])
        pieces.append(s3[i * tiles:] + cb_s[...].reshape(tiles, sub, rows))
        s3 = jnp.concatenate(pieces, axis=0)
        m = jnp.max(s3, axis=0)
        for shift in (4, 2, 1):
            m = jnp.maximum(m, pltpu.roll(m, shift, 0))
        p = jnp.exp(s3 - m[None]).reshape(n_keys, rows).astype(BF16)
        acc = _dot(vt_s[:, :n_keys], p)
        denom = acc[A_HEAD_DIM:A_HEAD_DIM + sub]
        out_t = (acc[:A_HEAD_DIM].reshape(A_HEAD_DIM // sub, sub, rows) / denom[None]).reshape(A_HEAD_DIM, rows)
        o_ref[0, i * A_BLOCK:(i + 1) * A_BLOCK, :] = jnp.concatenate(
            [out_t[:, g * A_BLOCK:(g + 1) * A_BLOCK].T for g in range(A_GROUP)], axis=1)


def _moba_prompt(qa, ka, va):
    gw = A_GROUP * A_HEAD_DIM
    return pl.pallas_call(
        _moba_prompt_kernel,
        grid=(BATCH, A_KV_HEADS),
        in_specs=[pl.BlockSpec((1, SEQ, gw), lambda b, g: (b, 0, g)),
                  pl.BlockSpec((1, SEQ, A_HEAD_DIM), lambda b, g: (b, 0, g)),
                  pl.BlockSpec((1, SEQ, A_HEAD_DIM), lambda b, g: (b, 0, g))],
        out_specs=pl.BlockSpec((1, SEQ, gw), lambda b, g: (b, 0, g)),
        out_shape=jax.ShapeDtypeStruct((BATCH, SEQ, A_WIDTH), F32),
        scratch_shapes=[pltpu.VMEM((SEQ, A_HEAD_DIM), BF16), pltpu.VMEM((VT_ROWS, SEQ), BF16),
                        pltpu.VMEM((A_BLOCK, A_GROUP * A_BLOCK), F32)],
        compiler_params=_params(("arbitrary", "arbitrary")),
        name="moba_prompt",
    )(qa, ka, va)


MLA_TQ = 128
MLA_TK = 256
MLA_ROW_CHUNK = 256


def _mla_prompt_kernel(ql_ref, qr_ref, ckv_ref, kr_ref, wuv_ref, o_ref,
                       ckvb_s, krb_s, qls_s, qrs_s, m_s, l_s, acc_s):
    i = pl.program_id(1)
    tq = MLA_TQ
    rc = MLA_ROW_CHUNK
    chunks = [slice(c * rc, (c + 1) * rc) for c in range(B_HEADS * tq // rc)]

    @pl.when(i == 0)
    def _():
        ckvb_s[...] = ckv_ref[0].astype(BF16)
        krb_s[...] = kr_ref[0].astype(BF16)

    ql = ql_ref[0]
    qr = qr_ref[0]
    for h in range(B_HEADS):
        qls_s[h * tq:(h + 1) * tq, :] = ql[:, h * B_KV_LORA:(h + 1) * B_KV_LORA].astype(BF16)
        qrs_s[h * tq:(h + 1) * tq, :] = qr[:, h * B_ROPE:(h + 1) * B_ROPE].astype(BF16)

    def scores(rs, ckv_blk, kr_blk):
        return (_dot_nt(qls_s[rs, :], ckv_blk) + _dot_nt(qrs_s[rs, :], kr_blk)) * B_SCALE

    jd = (i * tq) // MLA_TK
    dstart = pl.multiple_of(jd * MLA_TK, MLA_TK)
    ckv_d = ckvb_s[pl.ds(dstart, MLA_TK), :]
    kr_d = krb_s[pl.ds(dstart, MLA_TK), :]
    sds = [scores(rs, ckv_d, kr_d) for rs in chunks]
    pds = []
    for c, (rs, s) in enumerate(zip(chunks, sds)):
        qpos = i * tq + ((c * rc + lax.broadcasted_iota(jnp.int32, s.shape, 0)) & (tq - 1))
        kpos = jd * MLA_TK + lax.broadcasted_iota(jnp.int32, s.shape, 1)
        s = jnp.where(qpos >= kpos, s, NEG_INF)
        m = jnp.broadcast_to(jnp.max(s, axis=1, keepdims=True), (rc, LANES))
        p = jnp.exp(s - _lanes2(m))
        m_s[rs, :] = m
        l_s[rs, :] = jnp.broadcast_to(jnp.sum(p, axis=1, keepdims=True), (rc, LANES))
        pds.append(p.astype(BF16))
    for rs, p in zip(chunks, pds):
        acc_s[rs, :] = _dot(p, ckv_d)

    def body(j, carry):
        start = pl.multiple_of(j * MLA_TK, MLA_TK)
        ckv_j = ckvb_s[pl.ds(start, MLA_TK), :]
        kr_j = krb_s[pl.ds(start, MLA_TK), :]
        sjs = [scores(rs, ckv_j, kr_j) for rs in chunks]
        pjs, alphas = [], []
        for rs, sj in zip(chunks, sjs):
            m_old = m_s[rs, :]
            m_new = jnp.maximum(m_old, jnp.max(sj, axis=1, keepdims=True))
            alpha = jnp.exp(m_old - m_new)
            pj = jnp.exp(sj - _lanes2(m_new))
            l_s[rs, :] = alpha * l_s[rs, :] + jnp.sum(pj, axis=1, keepdims=True)
            m_s[rs, :] = m_new
            pjs.append(pj.astype(BF16))
            alphas.append(alpha)
        for rs, pj, alpha in zip(chunks, pjs, alphas):
            acc_s[rs, :] = _lanes2(alpha) * acc_s[rs, :] + _dot(pj, ckv_j)
        return carry

    lax.fori_loop(0, jd, body, 0)

    out = (acc_s[...] / _lanes2(l_s[...])).astype(BF16)
    o_ref[0] = jnp.concatenate([_dot(out[h * tq:(h + 1) * tq], wuv_ref[h]) for h in range(B_HEADS)], axis=1)


def _mla_prompt(q_lat, q_rope, ckv, kr, wuv_t):
    tq = MLA_TQ
    rows = B_HEADS * tq
    return pl.pallas_call(
        _mla_prompt_kernel,
        grid=(BATCH, SEQ // tq),
        in_specs=[pl.BlockSpec((1, tq, B_HEADS * B_KV_LORA), lambda b, i: (b, i, 0)),
                  pl.BlockSpec((1, tq, B_HEADS * B_ROPE), lambda b, i: (b, i, 0)),
                  pl.BlockSpec((1, SEQ, B_KV_LORA), lambda b, i: (b, 0, 0)),
                  pl.BlockSpec((1, SEQ, B_ROPE), lambda b, i: (b, 0, 0)),
                  _resident(wuv_t.shape)],
        out_specs=pl.BlockSpec((1, tq, B_WIDTH), lambda b, i: (b, i, 0)),
        out_shape=jax.ShapeDtypeStruct((BATCH, SEQ, B_WIDTH), F32),
        scratch_shapes=[pltpu.VMEM((SEQ, B_KV_LORA), BF16), pltpu.VMEM((SEQ, B_ROPE), BF16),
                        pltpu.VMEM((rows, B_KV_LORA), BF16), pltpu.VMEM((rows, B_ROPE), BF16),
                        pltpu.VMEM((rows, LANES), F32), pltpu.VMEM((rows, LANES), F32),
                        pltpu.VMEM((rows, B_KV_LORA), F32)],
        compiler_params=_params(("arbitrary", "arbitrary")),
        name="mla_prompt",
    )(q_lat, q_rope, ckv, kr, wuv_t)


SAMPLE_ROWS = A_HEADS * DEC_SEQ
KV_ROWS = A_GROUP * DEC_SEQ


def _stack_heads(x, n_heads, width):
    return jnp.concatenate([x[:, h * width:(h + 1) * width] for h in range(n_heads)], axis=0)


def _unstack_heads(x, n_heads, tokens):
    return jnp.concatenate([x[h * tokens:(h + 1) * tokens] for h in range(n_heads)], axis=1)


def _causal_new_tokens(shape):
    tq = lax.broadcasted_iota(jnp.int32, shape, 0) & (DEC_SEQ - 1)
    tk = lax.broadcasted_iota(jnp.int32, shape, 1)
    return tk <= tq


def _log2(n):
    assert n & (n - 1) == 0
    return n.bit_length() - 1


def _set_lane(acc, lane, idx, col):
    return jnp.where(lane == idx, col, acc)


MOBA_GROUP = 4
MOBA_GROUPS = N_BLOCKS_PAST // MOBA_GROUP
PAGE_ROWS = PAGE_SIZE * A_KV_HEADS
BLOCK_ROWS = A_BLOCK * A_KV_HEADS
RING_SLOTS = 4
RING_LOOKAHEAD = RING_SLOTS - 1


def _moba_sample_kernel(pt_ref, q_ref, kn_ref, vn_ref, ck_ref, cv_ref, o_ref, kbuf, vbuf, sems, opart_s):
    n = pl.program_id(0)
    total = pl.num_programs(0) * MOBA_GROUPS

    def group_copies(seq, grp, half):
        copies = []
        for u in range(MOBA_GROUP * PAGES_PER_BLOCK):
            page = pt_ref[seq, MOBA_GROUP * PAGES_PER_BLOCK * grp + u]
            rows = pl.ds(u * PAGE_ROWS, PAGE_ROWS)
            copies.append((pltpu.make_async_copy(ck_ref.at[page], kbuf.at[half, rows], sems.at[0, half]), 0))
            copies.append((pltpu.make_async_copy(cv_ref.at[page], vbuf.at[half, rows], sems.at[1, half]), 1))
        return copies

    @pl.when(n == 0)
    def _():
        for g0 in range(RING_LOOKAHEAD):
            for cp, prio in group_copies(0, g0, g0 % RING_SLOTS):
                cp.start(priority=prio)

    qf = _stack_heads(q_ref[0], A_HEADS, A_HEAD_DIM)
    qb = qf.astype(BF16)
    row_head = lax.shift_right_logical(lax.broadcasted_iota(jnp.int32, (SAMPLE_ROWS, BLOCK_ROWS), 0),
                                       _log2(KV_ROWS))
    col_head = lax.broadcasted_iota(jnp.int32, (SAMPLE_ROWS, BLOCK_ROWS), 1) & (A_KV_HEADS - 1)
    same_head = row_head == col_head
    lane = lax.broadcasted_iota(jnp.int32, (SAMPLE_ROWS, LANES), 1)

    def group_partials(j0, half, stats):
        m_all, l_all, sg_all = stats
        block_rows = [slice(b * BLOCK_ROWS, (b + 1) * BLOCK_ROWS) for b in range(MOBA_GROUP)]
        scores, gates = [], []
        for rows in block_rows:
            kf = kbuf[half, rows, :]
            ks = jnp.sum(kf.reshape(BLOCK_ROWS // 8, 8, A_HEAD_DIM), axis=0)
            km = (ks[:A_KV_HEADS] + ks[A_KV_HEADS:]) * (1.0 / A_BLOCK)
            km_rows = jnp.concatenate(
                [jnp.broadcast_to(km[g:g + 1], (KV_ROWS, A_HEAD_DIM)) for g in range(A_KV_HEADS)], axis=0)
            gates.append(jnp.sum(qf * km_rows, axis=1, keepdims=True))
            scores.append(_dot_nt(qb, kf.astype(BF16)))
        probs = []
        for b in range(MOBA_GROUP):
            s = jnp.where(same_head, scores[b] * A_SCALE, NEG_INF)
            m = jnp.max(s, axis=1, keepdims=True)
            p = jnp.exp(s - m)
            probs.append(p.astype(BF16))
            m_all = _set_lane(m_all, lane, j0 + b, m)
            l_all = _set_lane(l_all, lane, j0 + b, jnp.sum(p, axis=1, keepdims=True))
            sg_all = _set_lane(sg_all, lane, j0 + b, gates[b])
        for b, rows in enumerate(block_rows):
            opart_s[j0 + b] = _dot(probs[b], vbuf[half, rows, :].astype(BF16))
        return m_all, l_all, sg_all

    def ring_step(jj, stats):
        for slot in range(RING_SLOTS):
            grp = jj * RING_SLOTS + slot
            nxt = n * MOBA_GROUPS + grp + RING_LOOKAHEAD

            @pl.when(nxt < total)
            def _():
                for cp, prio in group_copies(lax.shift_right_logical(nxt, _log2(MOBA_GROUPS)),
                                             nxt & (MOBA_GROUPS - 1), (slot + RING_LOOKAHEAD) % RING_SLOTS):
                    cp.start(priority=prio)

            for cp, _ in group_copies(n, grp, slot):
                cp.wait()
            stats = group_partials(grp * MOBA_GROUP, slot, stats)
        return stats

    zeros = jnp.zeros((SAMPLE_ROWS, LANES), F32)
    mm, ll, sg = lax.fori_loop(0, MOBA_GROUPS // RING_SLOTS, ring_step, (zeros, zeros, zeros))

    rank = jnp.zeros(sg.shape, jnp.int32)
    for jp in range(N_BLOCKS_PAST):
        col = sg[:, jp:jp + 1]
        beats = (col > sg) | ((col == sg) & (lane > jp))
        rank = rank + jnp.where(beats, 1, 0)
    sel = (rank < A_TOPK) & (lane < N_BLOCKS_PAST)

    kn, vn = kn_ref[0], vn_ref[0]
    mo, lo, oo = [], [], []
    for g in range(A_KV_HEADS):
        rs = slice(g * KV_ROWS, (g + 1) * KV_ROWS)
        cs = slice(g * A_HEAD_DIM, (g + 1) * A_HEAD_DIM)
        s = _dot_nt(qb[rs], kn[:, cs].astype(BF16)) * A_SCALE
        s = jnp.where(_causal_new_tokens(s.shape), s, NEG_INF)
        m = jnp.max(s, axis=1, keepdims=True)
        p = jnp.exp(s - m)
        mo.append(m)
        lo.append(jnp.sum(p, axis=1, keepdims=True))
        oo.append(_dot(p.astype(BF16), vn[:, cs].astype(BF16)))
    m_own = jnp.concatenate(mo, axis=0)
    l_own = jnp.concatenate(lo, axis=0)
    o_own = jnp.concatenate(oo, axis=0)

    m_all = jnp.maximum(m_own, jnp.max(jnp.where(sel, mm, NEG_INF), axis=1, keepdims=True))
    w = jnp.where(sel, jnp.exp(mm - m_all), 0.0)
    w_own = jnp.exp(m_own - m_all)
    denom = jnp.sum(w * ll, axis=1, keepdims=True) + w_own * l_own
    acc = w_own * o_own
    for jb in range(N_BLOCKS_PAST):
        acc = acc + w[:, jb:jb + 1] * opart_s[jb]
    o_ref[0] = _unstack_heads(acc / denom, A_HEADS, DEC_SEQ)


def _moba_sample(page_table, qa, ka_new, va_new, cache_k2, cache_v2):
    assert MOBA_GROUPS % RING_SLOTS == 0

    def seq_spec(width):
        return pl.BlockSpec((1, DEC_SEQ, width), lambda n, pt: (n, 0, 0))

    grid_spec = pltpu.PrefetchScalarGridSpec(
        num_scalar_prefetch=1,
        grid=(DEC_BATCH,),
        in_specs=[seq_spec(A_WIDTH), seq_spec(A_KV_WIDTH), seq_spec(A_KV_WIDTH),
                  pl.BlockSpec(memory_space=pl.ANY), pl.BlockSpec(memory_space=pl.ANY)],
        out_specs=seq_spec(A_WIDTH),
        scratch_shapes=[pltpu.VMEM((RING_SLOTS, MOBA_GROUP * BLOCK_ROWS, A_HEAD_DIM), F32),
                        pltpu.VMEM((RING_SLOTS, MOBA_GROUP * BLOCK_ROWS, A_HEAD_DIM), F32),
                        pltpu.SemaphoreType.DMA((2, RING_SLOTS)),
                        pltpu.VMEM((N_BLOCKS_PAST, SAMPLE_ROWS, A_HEAD_DIM), F32)])
    return pl.pallas_call(
        _moba_sample_kernel,
        grid_spec=grid_spec,
        out_shape=jax.ShapeDtypeStruct((DEC_BATCH, DEC_SEQ, A_WIDTH), F32),
        compiler_params=_params(("arbitrary",)),
        name="moba_sample",
    )(page_table, qa, ka_new, va_new, cache_k2, cache_v2)


MLA_CHUNK_PAGES = 8
MLA_CHUNK_KEYS = MLA_CHUNK_PAGES * PAGE_SIZE
MLA_CHUNKS = N_PAGES // MLA_CHUNK_PAGES
MLA_SAMPLE_ROWS = B_HEADS * DEC_SEQ
MLA_SEQ_SLOTS = 2


def _mla_sample_kernel(pt_ref, ql_ref, qr_ref, cn_ref, rn_ref, wuv_ref, cc_ref, cr_ref, o_ref,
                       cbuf, rbuf, sems, opart_s):
    n = pl.program_id(0)
    slot = n & (MLA_SEQ_SLOTS - 1)

    def seq_copies(seq, slot):
        copies = []
        for u in range(N_PAGES):
            page = pt_ref[seq, u]
            copies.append((pltpu.make_async_copy(cc_ref.at[page], cbuf.at[slot, pl.ds(u * PAGE_SIZE, PAGE_SIZE)],
                                                 sems.at[0, slot]), u % 2))
            copies.append((pltpu.make_async_copy(cr_ref.at[page], rbuf.at[slot, u], sems.at[1, slot]), (u + 1) % 2))
        return copies

    @pl.when(n == 0)
    def _():
        for cp, prio in seq_copies(0, 0):
            cp.start(priority=prio)

    @pl.when(n + 1 < pl.num_programs(0))
    def _():
        for cp, prio in seq_copies(n + 1, (n + 1) & (MLA_SEQ_SLOTS - 1)):
            cp.start(priority=prio)

    qls = _stack_heads(ql_ref[0], B_HEADS, B_KV_LORA).astype(BF16)
    qrs = _stack_heads(qr_ref[0], B_HEADS, B_ROPE).astype(BF16)
    lane = lax.broadcasted_iota(jnp.int32, (MLA_SAMPLE_ROWS, LANES), 1)

    for cp, _ in seq_copies(n, slot):
        cp.wait()

    mm = jnp.zeros((MLA_SAMPLE_ROWS, LANES), F32)
    ll = jnp.zeros((MLA_SAMPLE_ROWS, LANES), F32)
    latents, scores = [], []
    for k in range(MLA_CHUNKS):
        cb = cbuf[slot, k * MLA_CHUNK_KEYS:(k + 1) * MLA_CHUNK_KEYS, :].astype(BF16)
        rt = jnp.concatenate([rbuf[slot, k * MLA_CHUNK_PAGES + u] for u in range(MLA_CHUNK_PAGES)],
                             axis=1).astype(BF16)
        latents.append(cb)
        scores.append(_dot_nt(qls, cb) + _dot(qrs, rt))
    probs = []
    for k in range(MLA_CHUNKS):
        s = scores[k] * B_SCALE
        m = jnp.max(s, axis=1, keepdims=True)
        p = jnp.exp(s - m)
        probs.append(p.astype(BF16))
        mm = _set_lane(mm, lane, k, m)
        ll = _set_lane(ll, lane, k, jnp.sum(p, axis=1, keepdims=True))
    for k in range(MLA_CHUNKS):
        opart_s[k] = _dot(probs[k], latents[k])

    cn = cn_ref[0].astype(BF16)
    sn = (_dot_nt(qls, cn) + _dot_nt(qrs, rn_ref[0].astype(BF16))) * B_SCALE
    sn = jnp.where(_causal_new_tokens(sn.shape), sn, NEG_INF)
    m_own = jnp.max(sn, axis=1, keepdims=True)
    p_own = jnp.exp(sn - m_own)
    l_own = jnp.sum(p_own, axis=1, keepdims=True)
    o_own = _dot(p_own.astype(BF16), cn)

    cached = lane < MLA_CHUNKS
    m_all = jnp.maximum(m_own, jnp.max(jnp.where(cached, mm, NEG_INF), axis=1, keepdims=True))
    w = jnp.where(cached, jnp.exp(mm - m_all), 0.0)
    w_own = jnp.exp(m_own - m_all)
    denom = jnp.sum(w * ll, axis=1, keepdims=True) + w_own * l_own
    acc = w_own * o_own
    for c in range(MLA_CHUNKS):
        acc = acc + w[:, c:c + 1] * opart_s[c]
    out = (acc / denom).astype(BF16)
    o_ref[0] = jnp.concatenate(
        [_dot(out[h * DEC_SEQ:(h + 1) * DEC_SEQ], wuv_ref[h]) for h in range(B_HEADS)], axis=1)


def _mla_sample(page_table, q_lat, q_rope, ckv_new, kr_new, wuv_t, cache_ckv2, cache_krt):
    assert MLA_SEQ_SLOTS & (MLA_SEQ_SLOTS - 1) == 0 and MLA_CHUNKS <= LANES

    def seq_spec(width):
        return pl.BlockSpec((1, DEC_SEQ, width), lambda n, pt: (n, 0, 0))

    grid_spec = pltpu.PrefetchScalarGridSpec(
        num_scalar_prefetch=1,
        grid=(DEC_BATCH,),
        in_specs=[seq_spec(B_HEADS * B_KV_LORA), seq_spec(B_HEADS * B_ROPE), seq_spec(B_KV_LORA),
                  seq_spec(B_ROPE),
                  pl.BlockSpec(wuv_t.shape, lambda n, pt: (0, 0, 0), pipeline_mode=pl.Buffered(1)),
                  pl.BlockSpec(memory_space=pl.ANY), pl.BlockSpec(memory_space=pl.ANY)],
        out_specs=seq_spec(B_WIDTH),
        scratch_shapes=[pltpu.VMEM((MLA_SEQ_SLOTS, PAST_LEN, B_KV_LORA), F32),
                        pltpu.VMEM((MLA_SEQ_SLOTS, N_PAGES, B_ROPE, PAGE_SIZE), F32),
                        pltpu.SemaphoreType.DMA((2, MLA_SEQ_SLOTS)),
                        pltpu.VMEM((MLA_CHUNKS, MLA_SAMPLE_ROWS, B_KV_LORA), F32)])
    return pl.pallas_call(
        _mla_sample_kernel,
        grid_spec=grid_spec,
        out_shape=jax.ShapeDtypeStruct((DEC_BATCH, DEC_SEQ, B_WIDTH), F32),
        compiler_params=_params(("arbitrary",)),
        name="mla_sample",
    )(page_table, q_lat, q_rope, ckv_new, kr_new, wuv_t, cache_ckv2, cache_krt)


def _out_proj_kernel(oa_ref, ga_ref, ob_ref, gb_ref, x_ref, gate_ref, w_ref, lng_ref, lnb_ref, y_ref,
                     *, per_row_mod):
    u = jnp.concatenate([oa_ref[...] * _silu(ga_ref[...]), ob_ref[...] * _silu(gb_ref[...])], axis=1)
    o = _dot(u.astype(BF16), w_ref[...])
    gate = gate_ref[...] if per_row_mod else gate_ref[0]
    r = ALPHA * x_ref[...] + gate * o
    mu = jnp.mean(r, axis=-1, keepdims=True)
    d = r - mu
    var = jnp.mean(d * d, axis=-1, keepdims=True)
    y_ref[...] = d * lax.rsqrt(var + LN_EPS) * lng_ref[...] + lnb_ref[...]


def _out_proj(oa, ga, ob, gb, x, mod, w_out, ln_g, ln_b, *, per_row_mod, rows_per_batch):
    rows = x.shape[0]
    tm = 256
    if per_row_mod:
        gate_spec = pl.BlockSpec((tm, D_MODEL), lambda i: (i, 2))
    else:
        tiles_per_batch = rows_per_batch // tm
        gate_spec = pl.BlockSpec((1, 1, D_MODEL), lambda i: (i // tiles_per_batch, 0, 2))
    half = pl.BlockSpec((tm, A_WIDTH), lambda i: (i, 0))
    full = pl.BlockSpec((tm, D_MODEL), lambda i: (i, 0))
    return pl.pallas_call(
        functools.partial(_out_proj_kernel, per_row_mod=per_row_mod),
        grid=(rows // tm,),
        in_specs=[half, half, half, half, full, gate_spec,
                  _resident(w_out.shape), _resident(ln_g.shape), _resident(ln_b.shape)],
        out_specs=full,
        out_shape=jax.ShapeDtypeStruct((rows, D_MODEL), F32),
        compiler_params=_params(("arbitrary",)),
        name="out_proj_sample" if per_row_mod else "out_proj_prompt",
    )(oa, ga, ob, gb, x, mod, w_out, ln_g, ln_b)


def _rope_tables(pos, half):
    inv = jnp.power(ROPE_THETA, -jnp.arange(half, dtype=F32) / half)
    ang = pos.astype(F32)[:, None] * inv[None, :]
    cos, sin = jnp.cos(ang), jnp.sin(ang)
    return jnp.concatenate([cos, cos], axis=1), jnp.concatenate([-sin, sin], axis=1)


def _rope_tables_128(pos):
    cosa, sina = _rope_tables(pos, A_HEAD_DIM // 2)
    cosb, sinb = _rope_tables(pos, B_ROPE // 2)
    return cosa, sina, jnp.tile(cosb, (1, 2)), jnp.tile(sinb, (1, 2))


def kernel(x_prompt, x_sample, cache_k, cache_v, cache_ckv, cache_kr, page_table, c_prompt, c_sample,
           w_ada, b_ada, w_in, q_norm_g, w_uq, kv_norm_g, w_uk, w_uv, w_out, ln_g, ln_b):
    layer = 0
    n_pool = cache_k.shape[1]
    rows_p = BATCH * SEQ
    rows_s = DEC_BATCH * DEC_SEQ

    w = w_in[layer]
    w_p = jnp.concatenate([w[:, :3840], w[:, 3904:], w[:, 3840:3904], jnp.zeros((D_MODEL, 64), F32)],
                          axis=1).astype(BF16)
    wuq = w_uq[layer].reshape(B_Q_LORA, B_HEADS, B_NOPE + B_ROPE)
    wuq_n = wuq[:, :, :B_NOPE].reshape(B_Q_LORA, B_HEADS * B_NOPE).astype(BF16)
    wuq_r = wuq[:, :, B_NOPE:].reshape(B_Q_LORA, B_HEADS * B_ROPE).astype(BF16)
    wuk_t = jnp.transpose(w_uk[layer], (1, 2, 0)).astype(BF16)
    wuv_t = jnp.transpose(w_uv[layer], (1, 0, 2)).astype(BF16)
    w_o = w_out[layer].astype(BF16)
    qg, kvg, lng, lnb = q_norm_g[layer][None], kv_norm_g[layer][None], ln_g[layer][None], ln_b[layer][None]

    n_c = BATCH + DEC_BATCH
    pad = (-n_c) % 8
    c_all = jnp.concatenate([c_prompt, c_sample, jnp.zeros((pad, D_MODEL), F32)], axis=0)
    mod = _ada_mod(c_all, w_ada[layer], b_ada[layer][None])
    mod_p = mod[:BATCH].reshape(BATCH, 1, 3 * D_MODEL)
    mod_s = jnp.repeat(mod[BATCH:n_c], DEC_SEQ, axis=0)

    tabs_p = _rope_tables_128(jnp.arange(SEQ, dtype=jnp.int32))
    tm = 256
    tabs_s = tuple(jnp.tile(t, (tm // DEC_SEQ, 1))
                   for t in _rope_tables_128(PAST_LEN + jnp.arange(DEC_SEQ, dtype=jnp.int32)))

    xp = x_prompt.reshape(rows_p, D_MODEL)
    xs = x_sample.reshape(rows_s, D_MODEL)
    proj_w = (w_p, wuq_n, wuq_r, wuk_t, qg, kvg)
    qa_p, ka_p, va_p, ga_p, ql_p, qr_p, ckv_p, kr_p, gb_p = _in_proj(
        xp, mod_p, *proj_w, tabs_p, per_row_mod=False, rows_per_batch=SEQ)
    qa_s, ka_s, va_s, ga_s, ql_s, qr_s, ckv_s, kr_s, gb_s = _in_proj(
        xs, mod_s, *proj_w, tabs_s, per_row_mod=True, rows_per_batch=DEC_SEQ)

    oa_p = _moba_prompt(qa_p.reshape(BATCH, SEQ, A_WIDTH), ka_p.reshape(BATCH, SEQ, A_KV_WIDTH),
                        va_p.reshape(BATCH, SEQ, A_KV_WIDTH))
    ob_p = _mla_prompt(ql_p.reshape(BATCH, SEQ, -1), qr_p.reshape(BATCH, SEQ, -1),
                       ckv_p.reshape(BATCH, SEQ, B_KV_LORA), kr_p.reshape(BATCH, SEQ, B_ROPE), wuv_t)
    y_p = _out_proj(oa_p.reshape(rows_p, A_WIDTH), ga_p, ob_p.reshape(rows_p, B_WIDTH), gb_p, xp, mod_p,
                    w_o, lng, lnb, per_row_mod=False, rows_per_batch=SEQ)

    cache_k2 = cache_k[layer].reshape(n_pool, PAGE_ROWS, A_HEAD_DIM)
    cache_v2 = cache_v[layer].reshape(n_pool, PAGE_ROWS, A_HEAD_DIM)
    cache_krt = jnp.swapaxes(cache_kr[layer], 1, 2)
    oa_s = _moba_sample(page_table, qa_s.reshape(DEC_BATCH, DEC_SEQ, A_WIDTH),
                        ka_s.reshape(DEC_BATCH, DEC_SEQ, A_KV_WIDTH), va_s.reshape(DEC_BATCH, DEC_SEQ, A_KV_WIDTH),
                        cache_k2, cache_v2)
    ob_s = _mla_sample(page_table, ql_s.reshape(DEC_BATCH, DEC_SEQ, -1), qr_s.reshape(DEC_BATCH, DEC_SEQ, -1),
                       ckv_s.reshape(DEC_BATCH, DEC_SEQ, B_KV_LORA), kr_s.reshape(DEC_BATCH, DEC_SEQ, B_ROPE),
                       wuv_t, cache_ckv[layer], cache_krt)
    y_s = _out_proj(oa_s.reshape(rows_s, A_WIDTH), ga_s, ob_s.reshape(rows_s, B_WIDTH), gb_s, xs, mod_s,
                    w_o, lng, lnb, per_row_mod=True, rows_per_batch=DEC_SEQ)

    return (y_p.reshape(BATCH, SEQ, D_MODEL), y_s.reshape(DEC_BATCH, DEC_SEQ, D_MODEL),
            ka_p.reshape(DEPTH, BATCH, SEQ, A_KV_HEADS, A_HEAD_DIM),
            va_p.reshape(DEPTH, BATCH, SEQ, A_KV_HEADS, A_HEAD_DIM),
            ckv_p.reshape(DEPTH, BATCH, SEQ, B_KV_LORA), kr_p.reshape(DEPTH, BATCH, SEQ, B_ROPE),
            ka_s.reshape(DEPTH, DEC_BATCH, DEC_SEQ, A_KV_HEADS, A_HEAD_DIM),
            va_s.reshape(DEPTH, DEC_BATCH, DEC_SEQ, A_KV_HEADS, A_HEAD_DIM),
            ckv_s.reshape(DEPTH, DEC_BATCH, DEC_SEQ, B_KV_LORA), kr_s.reshape(DEPTH, DEC_BATCH, DEC_SEQ, B_ROPE))
```

```python
import functools

import jax
import jax.numpy as jnp
from jax import lax
from jax.experimental import pallas as pl
from jax.experimental.pallas import tpu as pltpu

F32 = jnp.float32
BF16 = jnp.bfloat16
NEG_INF = float("-inf")

D_MODEL = 2048
BATCH = 4
SEQ = 2048
DEC_BATCH = 128
DEC_SEQ = 8
PAST_LEN = 8192
PAGE_SIZE = 128
N_PAGES = PAST_LEN // PAGE_SIZE

A_HEAD_DIM = 128
A_HEADS = 8
A_KV_HEADS = 4
A_GROUP = A_HEADS // A_KV_HEADS
A_WIDTH = A_HEADS * A_HEAD_DIM
A_KV_WIDTH = A_KV_HEADS * A_HEAD_DIM
A_BLOCK = 256
A_TOPK = 3
A_SCALE = A_HEAD_DIM ** -0.5
B_NOPE = 128
B_ROPE = 64
B_VDIM = 128
B_HEADS = 8
B_WIDTH = B_HEADS * B_VDIM
B_Q_LORA = 512
B_KV_LORA = 256
B_SCALE = (B_NOPE + B_ROPE) ** -0.5
ROPE_THETA = 10000.0
RMS_EPS = 1e-6
LN_EPS = 1e-5
DEPTH = 1
ALPHA = (2 * DEPTH) ** 0.25

N_BLOCKS_PROMPT = SEQ // A_BLOCK
N_BLOCKS_PAST = PAST_LEN // A_BLOCK
PAGES_PER_BLOCK = A_BLOCK // PAGE_SIZE

C_QA = (0, 1024)
C_KA = (1024, 1536)
C_VA = (1536, 2048)
C_GA = (2048, 3072)
C_CQ = (3072, 3584)
C_CKV = (3584, 3840)
C_GB = (3840, 4864)
C_KR = (4864, 4992)
W_IN_COLS = 4992

LANES = 128
VMEM_LIMIT_BYTES = 56 * 1024 * 1024


def _params(semantics):
    return pltpu.CompilerParams(dimension_semantics=semantics, vmem_limit_bytes=VMEM_LIMIT_BYTES)


def _silu(x):
    return x * jax.nn.sigmoid(x)


def _dot(a, b):
    return jnp.dot(a, b, preferred_element_type=F32)


def _dot_nt(a, b):
    return lax.dot_general(a, b, (((1,), (1,)), ((), ())), preferred_element_type=F32)


def _resident(shape):
    nd = len(shape)
    return pl.BlockSpec(shape, lambda *_: (0,) * nd, pipeline_mode=pl.Buffered(1))


def _ada_kernel(c_ref, w_ref, b_ref, o_ref):
    a = _silu(c_ref[...]).astype(BF16)
    o_ref[...] = _dot(a, w_ref[...].astype(BF16)) + b_ref[...]


def _ada_mod(c_all, w_ada, b_ada):
    rows = c_all.shape[0]
    n = w_ada.shape[1]
    tn = 768
    return pl.pallas_call(
        _ada_kernel,
        grid=(n // tn,),
        in_specs=[pl.BlockSpec((rows, D_MODEL), lambda j: (0, 0)),
                  pl.BlockSpec((D_MODEL, tn), lambda j: (0, j)),
                  pl.BlockSpec((1, tn), lambda j: (0, j))],
        out_specs=pl.BlockSpec((rows, tn), lambda j: (0, j)),
        out_shape=jax.ShapeDtypeStruct((rows, n), F32),
        compiler_params=_params(("arbitrary",)),
        name="ada_mod",
    )(c_all, w_ada, b_ada)


def _in_proj_kernel(x_ref, shift_ref, scale_ref, w_ref, wuqn_ref, wuqr_ref, wuk_ref, qg_ref, kvg_ref,
                    cosa_ref, sina_ref, cosb_ref, sinb_ref,
                    qa_ref, ka_ref, va_ref, ga_ref, qlat_ref, qrope_ref, ckv_ref, kr_ref, gb_ref,
                    *, per_row_mod):
    x = x_ref[...]
    if per_row_mod:
        shift, scale = shift_ref[...], scale_ref[...]
    else:
        shift, scale = shift_ref[0], scale_ref[0]
    hb = (x * (1.0 + scale) + shift).astype(BF16)
    tm = x.shape[0]
    cosa, sina = cosa_ref[...], sina_ref[...]
    cosb, sinb = cosb_ref[...], sinb_ref[...]

    def proj(cols):
        return _dot(hb, w_ref[:, cols[0]:cols[1]])

    def rope_full(z, n_heads):
        outs = []
        for h in range(n_heads):
            zh = z[:, h * LANES:(h + 1) * LANES]
            outs.append(zh * cosa + pltpu.roll(zh, 64, 1) * sina)
        return jnp.concatenate(outs, axis=1)

    lane = lax.broadcasted_iota(jnp.int32, (tm, LANES), 1)
    first_half = (lane & 63) < 32

    def rope_half(zg):
        partner = jnp.where(first_half, pltpu.roll(zg, 96, 1), pltpu.roll(zg, 32, 1))
        return zg * cosb + partner * sinb

    def rms(z, g):
        return z * lax.rsqrt(jnp.mean(z * z, axis=-1, keepdims=True) + RMS_EPS) * g

    qa_ref[...] = rope_full(proj(C_QA), A_HEADS)
    ka = rope_full(proj(C_KA), A_KV_HEADS)
    va = proj(C_VA)
    for g in range(A_KV_HEADS):
        ka_ref[pl.ds(g, tm, stride=A_KV_HEADS), :] = ka[:, g * A_HEAD_DIM:(g + 1) * A_HEAD_DIM]
        va_ref[pl.ds(g, tm, stride=A_KV_HEADS), :] = va[:, g * A_HEAD_DIM:(g + 1) * A_HEAD_DIM]
    ga_ref[...] = proj(C_GA)
    gb_ref[...] = proj(C_GB)
    ckv_ref[...] = rms(proj(C_CKV), kvg_ref[...])
    kr_ref[...] = rope_half(proj(C_KR))[:, :B_ROPE]

    cqn = rms(proj(C_CQ), qg_ref[...]).astype(BF16)
    qn = _dot(cqn, wuqn_ref[...])
    qr = _dot(cqn, wuqr_ref[...])
    for h in range(B_HEADS):
        qlat_ref[:, h * B_KV_LORA:(h + 1) * B_KV_LORA] = _dot(
            qn[:, h * B_NOPE:(h + 1) * B_NOPE].astype(BF16), wuk_ref[h])
    qrope_ref[...] = jnp.concatenate(
        [rope_half(qr[:, g * LANES:(g + 1) * LANES]) for g in range(B_HEADS * B_ROPE // LANES)], axis=1)


def _in_proj(x, mod, w_p, wuq_n, wuq_r, wuk_t, q_norm_g, kv_norm_g, tabs, *, per_row_mod, rows_per_batch):
    rows = x.shape[0]
    tm = 256
    steps = rows // tm
    if per_row_mod:
        shift_spec = pl.BlockSpec((tm, D_MODEL), lambda i: (i, 0))
        scale_spec = pl.BlockSpec((tm, D_MODEL), lambda i: (i, 1))
        tab_spec = pl.BlockSpec((tm, LANES), lambda i: (0, 0))
    else:
        tiles_per_batch = rows_per_batch // tm
        shift_spec = pl.BlockSpec((1, 1, D_MODEL), lambda i: (i // tiles_per_batch, 0, 0))
        scale_spec = pl.BlockSpec((1, 1, D_MODEL), lambda i: (i // tiles_per_batch, 0, 1))
        tab_spec = pl.BlockSpec((tm, LANES), lambda i: (i % tiles_per_batch, 0))
    outs = ((1, A_WIDTH), (A_KV_HEADS, A_HEAD_DIM), (A_KV_HEADS, A_HEAD_DIM), (1, A_WIDTH),
            (1, B_HEADS * B_KV_LORA), (1, B_HEADS * B_ROPE), (1, B_KV_LORA), (1, B_ROPE), (1, B_WIDTH))
    return pl.pallas_call(
        functools.partial(_in_proj_kernel, per_row_mod=per_row_mod),
        grid=(steps,),
        in_specs=[pl.BlockSpec((tm, D_MODEL), lambda i: (i, 0)), shift_spec, scale_spec,
                  _resident(w_p.shape), _resident(wuq_n.shape), _resident(wuq_r.shape),
                  _resident(wuk_t.shape), _resident(q_norm_g.shape), _resident(kv_norm_g.shape),
                  tab_spec, tab_spec, tab_spec, tab_spec],
        out_specs=[pl.BlockSpec((r * tm, w), lambda i: (i, 0)) for r, w in outs],
        out_shape=[jax.ShapeDtypeStruct((r * rows, w), F32) for r, w in outs],
        compiler_params=_params(("arbitrary",)),
        name="in_proj_sample" if per_row_mod else "in_proj_prompt",
    )(x, mod, mod, w_p, wuq_n, wuq_r, wuk_t, q_norm_g, kv_norm_g, *tabs)


ONES_ROWS = 16
VT_ROWS = A_HEAD_DIM + ONES_ROWS


def _moba_prompt_kernel(q_ref, k_ref, v_ref, o_ref, kb_s, vt_s, cb_s):
    kv_head = pl.program_id(1)
    k = k_ref[0, pl.ds(kv_head, SEQ, stride=A_KV_HEADS), :]
    kb_s[...] = k.astype(BF16)
    vt_s[:A_HEAD_DIM, :] = v_ref[0, pl.ds(kv_head, SEQ, stride=A_KV_HEADS), :].T.astype(BF16)
    vt_s[A_HEAD_DIM:, :] = jnp.ones((ONES_ROWS, SEQ), BF16)
    km = jnp.concatenate(
        [jnp.sum(k[j * A_BLOCK:(j + 1) * A_BLOCK], axis=0, keepdims=True) * (1.0 / A_BLOCK)
         for j in range(N_BLOCKS_PROMPT)], axis=0).astype(BF16)

    rows = A_GROUP * A_BLOCK
    sub = 8
    tiles = A_BLOCK // sub
    key = lax.broadcasted_iota(jnp.int32, (A_BLOCK, rows), 0)
    tok = lax.broadcasted_iota(jnp.int32, (A_BLOCK, rows), 1) & (A_BLOCK - 1)
    cb_s[...] = jnp.where(key <= tok, 0.0, NEG_INF)
    blk = lax.broadcasted_iota(jnp.int32, (N_BLOCKS_PROMPT, rows), 0)

    for i in range(N_BLOCKS_PROMPT):
        q2 = q_ref[0, i * A_BLOCK:(i + 1) * A_BLOCK, :]
        qb = jnp.concatenate([q2[:, g * A_HEAD_DIM:(g + 1) * A_HEAD_DIM] for g in range(A_GROUP)],
                             axis=0).astype(BF16)
        n_keys = (i + 1) * A_BLOCK
        s = _dot_nt(kb_s[:n_keys, :], qb) * A_SCALE
        s3 = s.reshape(n_keys // sub, sub, rows)
        pieces = []
        if i > 0:
            sg = _dot_nt(km, qb)
            rank = jnp.zeros(sg.shape, jnp.int32)
            for jp in range(i):
                row = sg[jp:jp + 1, :]
                beats = (row > sg) | ((row == sg) & (blk > jp))
                rank = rank + jnp.where(beats, 1, 0)
            gate_bias = jnp.where(rank < A_TOPK, 0.0, NEG_INF)
            for j in range(i):
                bias_j = jnp.broadcast_to(gate_bias[j:j + 1, :], (sub, rows))
                pieces.append(s3[j * tiles:(j + 1) * tiles] + bias_j[None])
        pieces.append(s3[i * tiles:] + cb_s[...].reshape(tiles, sub, rows))
        s3 = jnp.concatenate(pieces, axis=0)
        m = jnp.max(s3, axis=0)
        for shift in (4, 2, 1):
            m = jnp.maximum(m, pltpu.roll(m, shift, 0))
        p = jnp.exp(s3 - m[None]).reshape(n_keys, rows).astype(BF16)
        acc = _dot(vt_s[:, :n_keys], p)
        denom = acc[A_HEAD_DIM:A_HEAD_DIM + sub]
        out_t = (acc[:A_HEAD_DIM].reshape(A_HEAD_DIM // sub, sub, rows) / denom[None]).reshape(A_HEAD_DIM, rows)
        o_ref[0, i * A_BLOCK:(i + 1) * A_BLOCK, :] = jnp.concatenate(
            [out_t[:, g * A_BLOCK:(g + 1) * A_BLOCK].T for g in range(A_GROUP)], axis=1)


def _moba_prompt(qa, ka, va):
    gw = A_GROUP * A_HEAD_DIM
    kv_rows = SEQ * A_KV_HEADS
    return pl.pallas_call(
        _moba_prompt_kernel,
        grid=(BATCH, A_KV_HEADS),
        in_specs=[pl.BlockSpec((1, SEQ, gw), lambda b, g: (b, 0, g)),
                  pl.BlockSpec((1, kv_rows, A_HEAD_DIM), lambda b, g: (b, 0, 0)),
                  pl.BlockSpec((1, kv_rows, A_HEAD_DIM), lambda b, g: (b, 0, 0))],
        out_specs=pl.BlockSpec((1, SEQ, gw), lambda b, g: (b, 0, g)),
        out_shape=jax.ShapeDtypeStruct((BATCH, SEQ, A_WIDTH), F32),
        scratch_shapes=[pltpu.VMEM((SEQ, A_HEAD_DIM), BF16), pltpu.VMEM((VT_ROWS, SEQ), BF16),
                        pltpu.VMEM((A_BLOCK, A_GROUP * A_BLOCK), F32)],
        compiler_params=_params(("arbitrary", "arbitrary")),
        name="moba_prompt",
    )(qa, ka, va)


MLA_TQ = 128
MLA_TK = 256
MLA_ROW_CHUNK = 256


def _lanes2(x):
    return jnp.concatenate([x, x], axis=1)


def _mla_prompt_kernel(ql_ref, qr_ref, ckv_ref, kr_ref, wuv_ref, o_ref,
                       ckvb_s, krb_s, qls_s, qrs_s, m_s, l_s, acc_s):
    i = pl.program_id(1)
    tq = MLA_TQ
    rc = MLA_ROW_CHUNK
    chunks = [slice(c * rc, (c + 1) * rc) for c in range(B_HEADS * tq // rc)]

    @pl.when(i == 0)
    def _():
        ckvb_s[...] = ckv_ref[0].astype(BF16)
        krb_s[...] = kr_ref[0].astype(BF16)

    ql = ql_ref[0]
    qr = qr_ref[0]
    for h in range(B_HEADS):
        qls_s[h * tq:(h + 1) * tq, :] = ql[:, h * B_KV_LORA:(h + 1) * B_KV_LORA].astype(BF16)
        qrs_s[h * tq:(h + 1) * tq, :] = qr[:, h * B_ROPE:(h + 1) * B_ROPE].astype(BF16)

    def scores(rs, ckv_blk, kr_blk):
        return (_dot_nt(qls_s[rs, :], ckv_blk) + _dot_nt(qrs_s[rs, :], kr_blk)) * B_SCALE

    jd = (i * tq) // MLA_TK
    dstart = pl.multiple_of(jd * MLA_TK, MLA_TK)
    ckv_d = ckvb_s[pl.ds(dstart, MLA_TK), :]
    kr_d = krb_s[pl.ds(dstart, MLA_TK), :]
    sds = [scores(rs, ckv_d, kr_d) for rs in chunks]
    pds = []
    for c, (rs, s) in enumerate(zip(chunks, sds)):
        qpos = i * tq + ((c * rc + lax.broadcasted_iota(jnp.int32, s.shape, 0)) & (tq - 1))
        kpos = jd * MLA_TK + lax.broadcasted_iota(jnp.int32, s.shape, 1)
        s = jnp.where(qpos >= kpos, s, NEG_INF)
        m = jnp.broadcast_to(jnp.max(s, axis=1, keepdims=True), (rc, LANES))
        p = jnp.exp(s - _lanes2(m))
        m_s[rs, :] = m
        l_s[rs, :] = jnp.broadcast_to(jnp.sum(p, axis=1, keepdims=True), (rc, LANES))
        pds.append(p.astype(BF16))
    for rs, p in zip(chunks, pds):
        acc_s[rs, :] = _dot(p, ckv_d)

    def body(j, carry):
        start = pl.multiple_of(j * MLA_TK, MLA_TK)
        ckv_j = ckvb_s[pl.ds(start, MLA_TK), :]
        kr_j = krb_s[pl.ds(start, MLA_TK), :]
        sjs = [scores(rs, ckv_j, kr_j) for rs in chunks]
        pjs, alphas = [], []
        for rs, sj in zip(chunks, sjs):
            m_old = m_s[rs, :]
            m_new = jnp.maximum(m_old, jnp.max(sj, axis=1, keepdims=True))
            alpha = jnp.exp(m_old - m_new)
            pj = jnp.exp(sj - _lanes2(m_new))
            l_s[rs, :] = alpha * l_s[rs, :] + jnp.sum(pj, axis=1, keepdims=True)
            m_s[rs, :] = m_new
            pjs.append(pj.astype(BF16))
            alphas.append(alpha)
        for rs, pj, alpha in zip(chunks, pjs, alphas):
            acc_s[rs, :] = _lanes2(alpha) * acc_s[rs, :] + _dot(pj, ckv_j)
        return carry

    lax.fori_loop(0, jd, body, 0)

    out = (acc_s[...] / _lanes2(l_s[...])).astype(BF16)
    o_ref[0] = jnp.concatenate([_dot(out[h * tq:(h + 1) * tq], wuv_ref[h]) for h in range(B_HEADS)], axis=1)


def _mla_prompt(q_lat, q_rope, ckv, kr, wuv_t):
    tq = MLA_TQ
    rows = B_HEADS * tq
    return pl.pallas_call(
        _mla_prompt_kernel,
        grid=(BATCH, SEQ // tq),
        in_specs=[pl.BlockSpec((1, tq, B_HEADS * B_KV_LORA), lambda b, i: (b, i, 0)),
                  pl.BlockSpec((1, tq, B_HEADS * B_ROPE), lambda b, i: (b, i, 0)),
                  pl.BlockSpec((1, SEQ, B_KV_LORA), lambda b, i: (b, 0, 0)),
                  pl.BlockSpec((1, SEQ, B_ROPE), lambda b, i: (b, 0, 0)),
                  _resident(wuv_t.shape)],
        out_specs=pl.BlockSpec((1, tq, B_WIDTH), lambda b, i: (b, i, 0)),
        out_shape=jax.ShapeDtypeStruct((BATCH, SEQ, B_WIDTH), F32),
        scratch_shapes=[pltpu.VMEM((SEQ, B_KV_LORA), BF16), pltpu.VMEM((SEQ, B_ROPE), BF16),
                        pltpu.VMEM((rows, B_KV_LORA), BF16), pltpu.VMEM((rows, B_ROPE), BF16),
                        pltpu.VMEM((rows, LANES), F32), pltpu.VMEM((rows, LANES), F32),
                        pltpu.VMEM((rows, B_KV_LORA), F32)],
        compiler_params=_params(("arbitrary", "arbitrary")),
        name="mla_prompt",
    )(q_lat, q_rope, ckv, kr, wuv_t)


SAMPLE_ROWS = A_HEADS * DEC_SEQ
KV_ROWS = A_GROUP * DEC_SEQ


def _stack_heads(x, n_heads, width):
    return jnp.concatenate([x[:, h * width:(h + 1) * width] for h in range(n_heads)], axis=0)


def _unstack_heads(x, n_heads, tokens):
    return jnp.concatenate([x[h * tokens:(h + 1) * tokens] for h in range(n_heads)], axis=1)


def _causal_new_tokens(shape):
    tq = lax.broadcasted_iota(jnp.int32, shape, 0) & (DEC_SEQ - 1)
    tk = lax.broadcasted_iota(jnp.int32, shape, 1)
    return tk <= tq


def _log2(n):
    assert n & (n - 1) == 0
    return n.bit_length() - 1


def _set_lane(acc, lane, idx, col):
    return jnp.where(lane == idx, col, acc)


MOBA_GROUP = 4
MOBA_GROUPS = N_BLOCKS_PAST // MOBA_GROUP
PAGE_ROWS = PAGE_SIZE * A_KV_HEADS
BLOCK_ROWS = A_BLOCK * A_KV_HEADS
RING_SLOTS = 4
RING_LOOKAHEAD = RING_SLOTS - 1


def _moba_sample_kernel(pt_ref, q_ref, kn_ref, vn_ref, ck_ref, cv_ref, o_ref, kbuf, vbuf, sems, opart_s):
    n = pl.program_id(0)
    total = pl.num_programs(0) * MOBA_GROUPS

    def group_copies(seq, grp, half):
        copies = []
        for u in range(MOBA_GROUP * PAGES_PER_BLOCK):
            page = pt_ref[seq, MOBA_GROUP * PAGES_PER_BLOCK * grp + u]
            rows = pl.ds(u * PAGE_ROWS, PAGE_ROWS)
            copies.append((pltpu.make_async_copy(ck_ref.at[page], kbuf.at[half, rows], sems.at[0, half]), 0))
            copies.append((pltpu.make_async_copy(cv_ref.at[page], vbuf.at[half, rows], sems.at[1, half]), 1))
        return copies

    @pl.when(n == 0)
    def _():
        for g0 in range(RING_LOOKAHEAD):
            for cp, prio in group_copies(0, g0, g0 % RING_SLOTS):
                cp.start(priority=prio)

    qf = _stack_heads(q_ref[0], A_HEADS, A_HEAD_DIM)
    qb = qf.astype(BF16)
    row_head = lax.shift_right_logical(lax.broadcasted_iota(jnp.int32, (SAMPLE_ROWS, BLOCK_ROWS), 0),
                                       _log2(KV_ROWS))
    col_head = lax.broadcasted_iota(jnp.int32, (SAMPLE_ROWS, BLOCK_ROWS), 1) & (A_KV_HEADS - 1)
    same_head = row_head == col_head
    lane = lax.broadcasted_iota(jnp.int32, (SAMPLE_ROWS, LANES), 1)

    def group_partials(j0, half, stats):
        m_all, l_all, sg_all = stats
        block_rows = [slice(b * BLOCK_ROWS, (b + 1) * BLOCK_ROWS) for b in range(MOBA_GROUP)]
        scores, gates = [], []
        for rows in block_rows:
            kf = kbuf[half, rows, :]
            ks = jnp.sum(kf.reshape(BLOCK_ROWS // 8, 8, A_HEAD_DIM), axis=0)
            km = (ks[:A_KV_HEADS] + ks[A_KV_HEADS:]) * (1.0 / A_BLOCK)
            km_rows = jnp.concatenate(
                [jnp.broadcast_to(km[g:g + 1], (KV_ROWS, A_HEAD_DIM)) for g in range(A_KV_HEADS)], axis=0)
            gates.append(jnp.sum(qf * km_rows, axis=1, keepdims=True))
            scores.append(_dot_nt(qb, kf.astype(BF16)))
        probs = []
        for b in range(MOBA_GROUP):
            s = jnp.where(same_head, scores[b] * A_SCALE, NEG_INF)
            m = jnp.max(s, axis=1, keepdims=True)
            p = jnp.exp(s - m)
            probs.append(p.astype(BF16))
            m_all = _set_lane(m_all, lane, j0 + b, m)
            l_all = _set_lane(l_all, lane, j0 + b, jnp.sum(p, axis=1, keepdims=True))
            sg_all = _set_lane(sg_all, lane, j0 + b, gates[b])
        for b, rows in enumerate(block_rows):
            opart_s[j0 + b] = _dot(probs[b], vbuf[half, rows, :].astype(BF16))
        return m_all, l_all, sg_all

    def ring_step(jj, stats):
        for slot in range(RING_SLOTS):
            grp = jj * RING_SLOTS + slot
            nxt = n * MOBA_GROUPS + grp + RING_LOOKAHEAD

            @pl.when(nxt < total)
            def _():
                for cp, prio in group_copies(lax.shift_right_logical(nxt, _log2(MOBA_GROUPS)),
                                             nxt & (MOBA_GROUPS - 1), (slot + RING_LOOKAHEAD) % RING_SLOTS):
                    cp.start(priority=prio)

            for cp, _ in group_copies(n, grp, slot):
                cp.wait()
            stats = group_partials(grp * MOBA_GROUP, slot, stats)
        return stats

    zeros = jnp.zeros((SAMPLE_ROWS, LANES), F32)
    mm, ll, sg = lax.fori_loop(0, MOBA_GROUPS // RING_SLOTS, ring_step, (zeros, zeros, zeros))

    rank = jnp.zeros(sg.shape, jnp.int32)
    for jp in range(N_BLOCKS_PAST):
        col = sg[:, jp:jp + 1]
        beats = (col > sg) | ((col == sg) & (lane > jp))
        rank = rank + jnp.where(beats, 1, 0)
    sel = (rank < A_TOPK) & (lane < N_BLOCKS_PAST)

    mo, lo, oo = [], [], []
    for g in range(A_KV_HEADS):
        rs = slice(g * KV_ROWS, (g + 1) * KV_ROWS)
        kn = kn_ref[0, pl.ds(g, DEC_SEQ, stride=A_KV_HEADS), :]
        vn = vn_ref[0, pl.ds(g, DEC_SEQ, stride=A_KV_HEADS), :]
        s = _dot_nt(qb[rs], kn.astype(BF16)) * A_SCALE
        s = jnp.where(_causal_new_tokens(s.shape), s, NEG_INF)
        m = jnp.max(s, axis=1, keepdims=True)
        p = jnp.exp(s - m)
        mo.append(m)
        lo.append(jnp.sum(p, axis=1, keepdims=True))
        oo.append(_dot(p.astype(BF16), vn.astype(BF16)))
    m_own = jnp.concatenate(mo, axis=0)
    l_own = jnp.concatenate(lo, axis=0)
    o_own = jnp.concatenate(oo, axis=0)

    m_all = jnp.maximum(m_own, jnp.max(jnp.where(sel, mm, NEG_INF), axis=1, keepdims=True))
    w = jnp.where(sel, jnp.exp(mm - m_all), 0.0)
    w_own = jnp.exp(m_own - m_all)
    denom = jnp.sum(w * ll, axis=1, keepdims=True) + w_own * l_own
    acc = w_own * o_own
    for jb in range(N_BLOCKS_PAST):
        acc = acc + w[:, jb:jb + 1] * opart_s[jb]
    o_ref[0] = _unstack_heads(acc / denom, A_HEADS, DEC_SEQ)


def _moba_sample(page_table, qa, ka_new, va_new, cache_k2, cache_v2):
    assert MOBA_GROUPS % RING_SLOTS == 0

    def seq_spec(width, rows=DEC_SEQ):
        return pl.BlockSpec((1, rows, width), lambda n, pt: (n, 0, 0))

    new_kv_spec = seq_spec(A_HEAD_DIM, DEC_SEQ * A_KV_HEADS)
    grid_spec = pltpu.PrefetchScalarGridSpec(
        num_scalar_prefetch=1,
        grid=(DEC_BATCH,),
        in_specs=[seq_spec(A_WIDTH), new_kv_spec, new_kv_spec,
                  pl.BlockSpec(memory_space=pl.ANY), pl.BlockSpec(memory_space=pl.ANY)],
        out_specs=seq_spec(A_WIDTH),
        scratch_shapes=[pltpu.VMEM((RING_SLOTS, MOBA_GROUP * BLOCK_ROWS, A_HEAD_DIM), F32),
                        pltpu.VMEM((RING_SLOTS, MOBA_GROUP * BLOCK_ROWS, A_HEAD_DIM), F32),
                        pltpu.SemaphoreType.DMA((2, RING_SLOTS)),
                        pltpu.VMEM((N_BLOCKS_PAST, SAMPLE_ROWS, A_HEAD_DIM), F32)])
    return pl.pallas_call(
        _moba_sample_kernel,
        grid_spec=grid_spec,
        out_shape=jax.ShapeDtypeStruct((DEC_BATCH, DEC_SEQ, A_WIDTH), F32),
        compiler_params=_params(("arbitrary",)),
        name="moba_sample",
    )(page_table, qa, ka_new, va_new, cache_k2, cache_v2)


MLA_CHUNK_PAGES = 8
MLA_CHUNK_KEYS = MLA_CHUNK_PAGES * PAGE_SIZE
MLA_CHUNKS = N_PAGES // MLA_CHUNK_PAGES
MLA_SAMPLE_ROWS = B_HEADS * DEC_SEQ
MLA_SEQ_SLOTS = 2


def _mla_sample_kernel(pt_ref, ql_ref, qr_ref, cn_ref, rn_ref, wuv_ref, cc_ref, cr_ref, o_ref,
                       cbuf, rbuf, sems, opart_s):
    n = pl.program_id(0)
    slot = n & (MLA_SEQ_SLOTS - 1)

    def seq_copies(seq, slot):
        copies = []
        for u in range(N_PAGES):
            page = pt_ref[seq, u]
            copies.append((pltpu.make_async_copy(cc_ref.at[page], cbuf.at[slot, pl.ds(u * PAGE_SIZE, PAGE_SIZE)],
                                                 sems.at[0, slot]), u % 2))
            copies.append((pltpu.make_async_copy(cr_ref.at[page], rbuf.at[slot, u], sems.at[1, slot]), (u + 1) % 2))
        return copies

    @pl.when(n == 0)
    def _():
        for cp, prio in seq_copies(0, 0):
            cp.start(priority=prio)

    @pl.when(n + 1 < pl.num_programs(0))
    def _():
        for cp, prio in seq_copies(n + 1, (n + 1) & (MLA_SEQ_SLOTS - 1)):
            cp.start(priority=prio)

    qls = _stack_heads(ql_ref[0], B_HEADS, B_KV_LORA).astype(BF16)
    qrs = _stack_heads(qr_ref[0], B_HEADS, B_ROPE).astype(BF16)
    lane = lax.broadcasted_iota(jnp.int32, (MLA_SAMPLE_ROWS, LANES), 1)

    for cp, _ in seq_copies(n, slot):
        cp.wait()

    mm = jnp.zeros((MLA_SAMPLE_ROWS, LANES), F32)
    ll = jnp.zeros((MLA_SAMPLE_ROWS, LANES), F32)
    latents, scores = [], []
    for k in range(MLA_CHUNKS):
        cb = cbuf[slot, k * MLA_CHUNK_KEYS:(k + 1) * MLA_CHUNK_KEYS, :].astype(BF16)
        rt = jnp.concatenate([rbuf[slot, k * MLA_CHUNK_PAGES + u] for u in range(MLA_CHUNK_PAGES)],
                             axis=1).astype(BF16)
        latents.append(cb)
        scores.append(_dot_nt(qls, cb) + _dot(qrs, rt))
    probs = []
    for k in range(MLA_CHUNKS):
        s = scores[k] * B_SCALE
        m = jnp.max(s, axis=1, keepdims=True)
        p = jnp.exp(s - m)
        probs.append(p.astype(BF16))
        mm = _set_lane(mm, lane, k, m)
        ll = _set_lane(ll, lane, k, jnp.sum(p, axis=1, keepdims=True))
    for k in range(MLA_CHUNKS):
        opart_s[k] = _dot(probs[k], latents[k])

    cn = cn_ref[0].astype(BF16)
    sn = (_dot_nt(qls, cn) + _dot_nt(qrs, rn_ref[0].astype(BF16))) * B_SCALE
    sn = jnp.where(_causal_new_tokens(sn.shape), sn, NEG_INF)
    m_own = jnp.max(sn, axis=1, keepdims=True)
    p_own = jnp.exp(sn - m_own)
    l_own = jnp.sum(p_own, axis=1, keepdims=True)
    o_own = _dot(p_own.astype(BF16), cn)

    cached = lane < MLA_CHUNKS
    m_all = jnp.maximum(m_own, jnp.max(jnp.where(cached, mm, NEG_INF), axis=1, keepdims=True))
    w = jnp.where(cached, jnp.exp(mm - m_all), 0.0)
    w_own = jnp.exp(m_own - m_all)
    denom = jnp.sum(w * ll, axis=1, keepdims=True) + w_own * l_own
    acc = w_own * o_own
    for c in range(MLA_CHUNKS):
        acc = acc + w[:, c:c + 1] * opart_s[c]
    out = (acc / denom).astype(BF16)
    o_ref[0] = jnp.concatenate(
        [_dot(out[h * DEC_SEQ:(h + 1) * DEC_SEQ], wuv_ref[h]) for h in range(B_HEADS)], axis=1)


def _mla_sample(page_table, q_lat, q_rope, ckv_new, kr_new, wuv_t, cache_ckv2, cache_krt):
    assert MLA_SEQ_SLOTS & (MLA_SEQ_SLOTS - 1) == 0 and MLA_CHUNKS <= LANES

    def seq_spec(width):
        return pl.BlockSpec((1, DEC_SEQ, width), lambda n, pt: (n, 0, 0))

    grid_spec = pltpu.PrefetchScalarGridSpec(
        num_scalar_prefetch=1,
        grid=(DEC_BATCH,),
        in_specs=[seq_spec(B_HEADS * B_KV_LORA), seq_spec(B_HEADS * B_ROPE), seq_spec(B_KV_LORA),
                  seq_spec(B_ROPE),
                  pl.BlockSpec(wuv_t.shape, lambda n, pt: (0, 0, 0), pipeline_mode=pl.Buffered(1)),
                  pl.BlockSpec(memory_space=pl.ANY), pl.BlockSpec(memory_space=pl.ANY)],
        out_specs=seq_spec(B_WIDTH),
        scratch_shapes=[pltpu.VMEM((MLA_SEQ_SLOTS, PAST_LEN, B_KV_LORA), F32),
                        pltpu.VMEM((MLA_SEQ_SLOTS, N_PAGES, B_ROPE, PAGE_SIZE), F32),
                        pltpu.SemaphoreType.DMA((2, MLA_SEQ_SLOTS)),
                        pltpu.VMEM((MLA_CHUNKS, MLA_SAMPLE_ROWS, B_KV_LORA), F32)])
    return pl.pallas_call(
        _mla_sample_kernel,
        grid_spec=grid_spec,
        out_shape=jax.ShapeDtypeStruct((DEC_BATCH, DEC_SEQ, B_WIDTH), F32),
        compiler_params=_params(("arbitrary",)),
        name="mla_sample",
    )(page_table, q_lat, q_rope, ckv_new, kr_new, wuv_t, cache_ckv2, cache_krt)


def _out_proj_kernel(oa_ref, ga_ref, ob_ref, gb_ref, x_ref, gate_ref, w_ref, lng_ref, lnb_ref, y_ref,
                     *, per_row_mod):
    u = jnp.concatenate([oa_ref[...] * _silu(ga_ref[...]), ob_ref[...] * _silu(gb_ref[...])], axis=1)
    o = _dot(u.astype(BF16), w_ref[...])
    gate = gate_ref[...] if per_row_mod else gate_ref[0]
    r = ALPHA * x_ref[...] + gate * o
    mu = jnp.mean(r, axis=-1, keepdims=True)
    d = r - mu
    var = jnp.mean(d * d, axis=-1, keepdims=True)
    y_ref[...] = d * lax.rsqrt(var + LN_EPS) * lng_ref[...] + lnb_ref[...]


def _out_proj(oa, ga, ob, gb, x, mod, w_out, ln_g, ln_b, *, per_row_mod, rows_per_batch):
    rows = x.shape[0]
    tm = 256
    if per_row_mod:
        gate_spec = pl.BlockSpec((tm, D_MODEL), lambda i: (i, 2))
    else:
        tiles_per_batch = rows_per_batch // tm
        gate_spec = pl.BlockSpec((1, 1, D_MODEL), lambda i: (i // tiles_per_batch, 0, 2))
    half = pl.BlockSpec((tm, A_WIDTH), lambda i: (i, 0))
    full = pl.BlockSpec((tm, D_MODEL), lambda i: (i, 0))
    return pl.pallas_call(
        functools.partial(_out_proj_kernel, per_row_mod=per_row_mod),
        grid=(rows // tm,),
        in_specs=[half, half, half, half, full, gate_spec,
                  _resident(w_out.shape), _resident(ln_g.shape), _resident(ln_b.shape)],
        out_specs=full,
        out_shape=jax.ShapeDtypeStruct((rows, D_MODEL), F32),
        compiler_params=_params(("arbitrary",)),
        name="out_proj_sample" if per_row_mod else "out_proj_prompt",
    )(oa, ga, ob, gb, x, mod, w_out, ln_g, ln_b)


def _rope_tables(pos, half):
    inv = jnp.power(ROPE_THETA, -jnp.arange(half, dtype=F32) / half)
    ang = pos.astype(F32)[:, None] * inv[None, :]
    cos, sin = jnp.cos(ang), jnp.sin(ang)
    return jnp.concatenate([cos, cos], axis=1), jnp.concatenate([-sin, sin], axis=1)


def _rope_tables_128(pos):
    cosa, sina = _rope_tables(pos, A_HEAD_DIM // 2)
    cosb, sinb = _rope_tables(pos, B_ROPE // 2)
    return cosa, sina, jnp.tile(cosb, (1, 2)), jnp.tile(sinb, (1, 2))


def kernel(x_prompt, x_sample, cache_k, cache_v, cache_ckv, cache_kr, page_table, c_prompt, c_sample,
           w_ada, b_ada, w_in, q_norm_g, w_uq, kv_norm_g, w_uk, w_uv, w_out, ln_g, ln_b):
    layer = 0
    n_pool = cache_k.shape[1]
    rows_p = BATCH * SEQ
    rows_s = DEC_BATCH * DEC_SEQ

    w = w_in[layer]
    w_p = jnp.concatenate([w[:, :3840], w[:, 3904:], w[:, 3840:3904], jnp.zeros((D_MODEL, 64), F32)],
                          axis=1).astype(BF16)
    wuq = w_uq[layer].reshape(B_Q_LORA, B_HEADS, B_NOPE + B_ROPE)
    wuq_n = wuq[:, :, :B_NOPE].reshape(B_Q_LORA, B_HEADS * B_NOPE).astype(BF16)
    wuq_r = wuq[:, :, B_NOPE:].reshape(B_Q_LORA, B_HEADS * B_ROPE).astype(BF16)
    wuk_t = jnp.transpose(w_uk[layer], (1, 2, 0)).astype(BF16)
    wuv_t = jnp.transpose(w_uv[layer], (1, 0, 2)).astype(BF16)
    w_o = w_out[layer].astype(BF16)
    qg, kvg, lng, lnb = q_norm_g[layer][None], kv_norm_g[layer][None], ln_g[layer][None], ln_b[layer][None]

    n_c = BATCH + DEC_BATCH
    pad = (-n_c) % 8
    c_all = jnp.concatenate([c_prompt, c_sample, jnp.zeros((pad, D_MODEL), F32)], axis=0)
    mod = _ada_mod(c_all, w_ada[layer], b_ada[layer][None])
    mod_p = mod[:BATCH].reshape(BATCH, 1, 3 * D_MODEL)
    mod_s = jnp.repeat(mod[BATCH:n_c], DEC_SEQ, axis=0)

    tabs_p = _rope_tables_128(jnp.arange(SEQ, dtype=jnp.int32))
    tm = 256
    tabs_s = tuple(jnp.tile(t, (tm // DEC_SEQ, 1))
                   for t in _rope_tables_128(PAST_LEN + jnp.arange(DEC_SEQ, dtype=jnp.int32)))

    xp = x_prompt.reshape(rows_p, D_MODEL)
    xs = x_sample.reshape(rows_s, D_MODEL)
    proj_w = (w_p, wuq_n, wuq_r, wuk_t, qg, kvg)
    qa_p, ka_p, va_p, ga_p, ql_p, qr_p, ckv_p, kr_p, gb_p = _in_proj(
        xp, mod_p, *proj_w, tabs_p, per_row_mod=False, rows_per_batch=SEQ)
    qa_s, ka_s, va_s, ga_s, ql_s, qr_s, ckv_s, kr_s, gb_s = _in_proj(
        xs, mod_s, *proj_w, tabs_s, per_row_mod=True, rows_per_batch=DEC_SEQ)

    oa_p = _moba_prompt(qa_p.reshape(BATCH, SEQ, A_WIDTH), ka_p.reshape(BATCH, SEQ * A_KV_HEADS, A_HEAD_DIM),
                        va_p.reshape(BATCH, SEQ * A_KV_HEADS, A_HEAD_DIM))
    ob_p = _mla_prompt(ql_p.reshape(BATCH, SEQ, -1), qr_p.reshape(BATCH, SEQ, -1),
                       ckv_p.reshape(BATCH, SEQ, B_KV_LORA), kr_p.reshape(BATCH, SEQ, B_ROPE), wuv_t)
    y_p = _out_proj(oa_p.reshape(rows_p, A_WIDTH), ga_p, ob_p.reshape(rows_p, B_WIDTH), gb_p, xp, mod_p,
                    w_o, lng, lnb, per_row_mod=False, rows_per_batch=SEQ)

    cache_k2 = cache_k[layer].reshape(n_pool, PAGE_ROWS, A_HEAD_DIM)
    cache_v2 = cache_v[layer].reshape(n_pool, PAGE_ROWS, A_HEAD_DIM)
    cache_krt = jnp.swapaxes(cache_kr[layer], 1, 2)
    oa_s = _moba_sample(page_table, qa_s.reshape(DEC_BATCH, DEC_SEQ, A_WIDTH),
                        ka_s.reshape(DEC_BATCH, DEC_SEQ * A_KV_HEADS, A_HEAD_DIM),
                        va_s.reshape(DEC_BATCH, DEC_SEQ * A_KV_HEADS, A_HEAD_DIM), cache_k2, cache_v2)
    ob_s = _mla_sample(page_table, ql_s.reshape(DEC_BATCH, DEC_SEQ, -1), qr_s.reshape(DEC_BATCH, DEC_SEQ, -1),
                       ckv_s.reshape(DEC_BATCH, DEC_SEQ, B_KV_LORA), kr_s.reshape(DEC_BATCH, DEC_SEQ, B_ROPE),
                       wuv_t, cache_ckv[layer], cache_krt)
    y_s = _out_proj(oa_s.reshape(rows_s, A_WIDTH), ga_s, ob_s.reshape(rows_s, B_WIDTH), gb_s, xs, mod_s,
                    w_o, lng, lnb, per_row_mod=True, rows_per_batch=DEC_SEQ)

    return (y_p.reshape(BATCH, SEQ, D_MODEL), y_s.reshape(DEC_BATCH, DEC_SEQ, D_MODEL),
            ka_p.reshape(DEPTH, BATCH, SEQ, A_KV_HEADS, A_HEAD_DIM),
            va_p.reshape(DEPTH, BATCH, SEQ, A_KV_HEADS, A_HEAD_DIM),
            ckv_p.reshape(DEPTH, BATCH, SEQ, B_KV_LORA), kr_p.reshape(DEPTH, BATCH, SEQ, B_ROPE),
            ka_s.reshape(DEPTH, DEC_BATCH, DEC_SEQ, A_KV_HEADS, A_HEAD_DIM),
            va_s.reshape(DEPTH, DEC_BATCH, DEC_SEQ, A_KV_HEADS, A_HEAD_DIM),
            ckv_s.reshape(DEPTH, DEC_BATCH, DEC_SEQ, B_KV_LORA), kr_s.reshape(DEPTH, DEC_BATCH, DEC_SEQ, B_ROPE))
```

```python
import functools

import jax
import jax.numpy as jnp
from jax import lax
from jax.experimental import pallas as pl
from jax.experimental.pallas import tpu as pltpu

F32 = jnp.float32
BF16 = jnp.bfloat16
NEG_INF = float("-inf")

D_MODEL = 2048
BATCH = 4
SEQ = 2048
DEC_BATCH = 128
DEC_SEQ = 8
PAST_LEN = 8192
PAGE_SIZE = 128
N_PAGES = PAST_LEN // PAGE_SIZE

A_HEAD_DIM = 128
A_HEADS = 8
A_KV_HEADS = 4
A_GROUP = A_HEADS // A_KV_HEADS
A_WIDTH = A_HEADS * A_HEAD_DIM
A_KV_WIDTH = A_KV_HEADS * A_HEAD_DIM
A_BLOCK = 256
A_TOPK = 3
A_SCALE = A_HEAD_DIM ** -0.5
B_NOPE = 128
B_ROPE = 64
B_VDIM = 128
B_HEADS = 8
B_WIDTH = B_HEADS * B_VDIM
B_Q_LORA = 512
B_KV_LORA = 256
B_SCALE = (B_NOPE + B_ROPE) ** -0.5
ROPE_THETA = 10000.0
RMS_EPS = 1e-6
LN_EPS = 1e-5
DEPTH = 1
ALPHA = (2 * DEPTH) ** 0.25

N_BLOCKS_PROMPT = SEQ // A_BLOCK
N_BLOCKS_PAST = PAST_LEN // A_BLOCK
PAGES_PER_BLOCK = A_BLOCK // PAGE_SIZE

C_QA = (0, 1024)
C_KA = (1024, 1536)
C_VA = (1536, 2048)
C_GA = (2048, 3072)
C_CQ = (3072, 3584)
C_CKV = (3584, 3840)
C_TAIL = (3840, 4992)
IN_TOTAL = 4928
W_IN_COLS = 4992

LANES = 128
VMEM_LIMIT_BYTES = 56 * 1024 * 1024


def _params(semantics):
    return pltpu.CompilerParams(dimension_semantics=semantics, vmem_limit_bytes=VMEM_LIMIT_BYTES)


def _silu(x):
    return x * jax.nn.sigmoid(x)


def _dot(a, b):
    return jnp.dot(a, b, preferred_element_type=F32)


def _dot_nt(a, b):
    return lax.dot_general(a, b, (((1,), (1,)), ((), ())), preferred_element_type=F32)


def _resident(shape):
    nd = len(shape)
    return pl.BlockSpec(shape, lambda *_: (0,) * nd, pipeline_mode=pl.Buffered(1))


def _ada_kernel(c_ref, w_ref, b_ref, o_ref):
    a = _silu(c_ref[...]).astype(BF16)
    o_ref[...] = _dot(a, w_ref[...].astype(BF16)) + b_ref[...]


def _ada_mod(c_all, w_ada, b_ada):
    rows = c_all.shape[0]
    n = w_ada.shape[1]
    tn = 768
    return pl.pallas_call(
        _ada_kernel,
        grid=(n // tn,),
        in_specs=[pl.BlockSpec((rows, D_MODEL), lambda j: (0, 0)),
                  pl.BlockSpec((D_MODEL, tn), lambda j: (0, j)),
                  pl.BlockSpec((1, tn), lambda j: (0, j))],
        out_specs=pl.BlockSpec((rows, tn), lambda j: (0, j)),
        out_shape=jax.ShapeDtypeStruct((rows, n), F32),
        compiler_params=_params(("arbitrary",)),
        name="ada_mod",
    )(c_all, w_ada, b_ada)


def _in_proj_kernel(x_ref, shift_ref, scale_ref, w_ref, wuqn_ref, wuqr_ref, wuk_ref, qg_ref, kvg_ref,
                    cosa_ref, sina_ref, cosb_ref, sinb_ref,
                    qa_ref, ka_ref, va_ref, ga_ref, qlat_ref, qrope_ref, ckv_ref, kr_ref, gb_ref,
                    *, per_row_mod):
    x = x_ref[...]
    if per_row_mod:
        shift, scale = shift_ref[...], scale_ref[...]
    else:
        shift, scale = shift_ref[0], scale_ref[0]
    hb = (x * (1.0 + scale) + shift).astype(BF16)
    tm = x.shape[0]
    cosa, sina = cosa_ref[...], sina_ref[...]
    cosb, sinb = cosb_ref[...], sinb_ref[...]

    def proj(cols):
        return _dot(hb, w_ref[:, cols[0]:cols[1]])

    def rope_full(z, n_heads):
        outs = []
        for h in range(n_heads):
            zh = z[:, h * LANES:(h + 1) * LANES]
            outs.append(zh * cosa + pltpu.roll(zh, 64, 1) * sina)
        return jnp.concatenate(outs, axis=1)

    lane = lax.broadcasted_iota(jnp.int32, (tm, LANES), 1)
    first_half = (lane & 63) < 32

    def rope_half(zg):
        partner = jnp.where(first_half, pltpu.roll(zg, 96, 1), pltpu.roll(zg, 32, 1))
        return zg * cosb + partner * sinb

    def rms(z, g):
        return z * lax.rsqrt(jnp.mean(z * z, axis=-1, keepdims=True) + RMS_EPS) * g

    qa_ref[...] = rope_full(proj(C_QA), A_HEADS)
    ka = rope_full(proj(C_KA), A_KV_HEADS)
    va = proj(C_VA)
    for g in range(A_KV_HEADS):
        ka_ref[pl.ds(g, tm, stride=A_KV_HEADS), :] = ka[:, g * A_HEAD_DIM:(g + 1) * A_HEAD_DIM]
        va_ref[pl.ds(g, tm, stride=A_KV_HEADS), :] = va[:, g * A_HEAD_DIM:(g + 1) * A_HEAD_DIM]
    ga_ref[...] = proj(C_GA)
    ckv_ref[...] = rms(proj(C_CKV), kvg_ref[...])
    tail = proj(C_TAIL)
    kr_ref[...] = rope_half(tail[:, :LANES])[:, :B_ROPE]
    gb_ref[...] = tail[:, B_ROPE:B_ROPE + B_WIDTH]

    cqn = rms(proj(C_CQ), qg_ref[...]).astype(BF16)
    qn = _dot(cqn, wuqn_ref[...])
    qr = _dot(cqn, wuqr_ref[...])
    for h in range(B_HEADS):
        qlat_ref[:, h * B_KV_LORA:(h + 1) * B_KV_LORA] = _dot(
            qn[:, h * B_NOPE:(h + 1) * B_NOPE].astype(BF16), wuk_ref[h])
    qrope_ref[...] = jnp.concatenate(
        [rope_half(qr[:, g * LANES:(g + 1) * LANES]) for g in range(B_HEADS * B_ROPE // LANES)], axis=1)


def _in_proj(x, mod, w_p, wuq_n, wuq_r, wuk_t, q_norm_g, kv_norm_g, tabs, *, per_row_mod, rows_per_batch):
    rows = x.shape[0]
    tm = 256
    steps = rows // tm
    if per_row_mod:
        shift_spec = pl.BlockSpec((tm, D_MODEL), lambda i: (i, 0))
        scale_spec = pl.BlockSpec((tm, D_MODEL), lambda i: (i, 1))
        tab_spec = pl.BlockSpec((tm, LANES), lambda i: (0, 0))
    else:
        tiles_per_batch = rows_per_batch // tm
        shift_spec = pl.BlockSpec((1, 1, D_MODEL), lambda i: (i // tiles_per_batch, 0, 0))
        scale_spec = pl.BlockSpec((1, 1, D_MODEL), lambda i: (i // tiles_per_batch, 0, 1))
        tab_spec = pl.BlockSpec((tm, LANES), lambda i: (i % tiles_per_batch, 0))
    outs = ((1, A_WIDTH), (A_KV_HEADS, A_HEAD_DIM), (A_KV_HEADS, A_HEAD_DIM), (1, A_WIDTH),
            (1, B_HEADS * B_KV_LORA), (1, B_HEADS * B_ROPE), (1, B_KV_LORA), (1, B_ROPE), (1, B_WIDTH))
    return pl.pallas_call(
        functools.partial(_in_proj_kernel, per_row_mod=per_row_mod),
        grid=(steps,),
        in_specs=[pl.BlockSpec((tm, D_MODEL), lambda i: (i, 0)), shift_spec, scale_spec,
                  _resident(w_p.shape), _resident(wuq_n.shape), _resident(wuq_r.shape),
                  _resident(wuk_t.shape), _resident(q_norm_g.shape), _resident(kv_norm_g.shape),
                  tab_spec, tab_spec, tab_spec, tab_spec],
        out_specs=[pl.BlockSpec((r * tm, w), lambda i: (i, 0)) for r, w in outs],
        out_shape=[jax.ShapeDtypeStruct((r * rows, w), F32) for r, w in outs],
        compiler_params=_params(("arbitrary",)),
        name="in_proj_sample" if per_row_mod else "in_proj_prompt",
    )(x, mod, mod, w_p, wuq_n, wuq_r, wuk_t, q_norm_g, kv_norm_g, *tabs)


ONES_ROWS = 16
VT_ROWS = A_HEAD_DIM + ONES_ROWS


def _moba_prompt_kernel(q_ref, k_ref, v_ref, o_ref, kb_s, vt_s, cb_s):
    kv_head = pl.program_id(1)
    k = k_ref[0, pl.ds(kv_head, SEQ, stride=A_KV_HEADS), :]
    kb_s[...] = k.astype(BF16)
    vt_s[:A_HEAD_DIM, :] = v_ref[0, pl.ds(kv_head, SEQ, stride=A_KV_HEADS), :].T.astype(BF16)
    vt_s[A_HEAD_DIM:, :] = jnp.ones((ONES_ROWS, SEQ), BF16)
    km = jnp.concatenate(
        [jnp.sum(k[j * A_BLOCK:(j + 1) * A_BLOCK], axis=0, keepdims=True) * (1.0 / A_BLOCK)
         for j in range(N_BLOCKS_PROMPT)], axis=0).astype(BF16)

    rows = A_GROUP * A_BLOCK
    sub = 8
    tiles = A_BLOCK // sub
    key = lax.broadcasted_iota(jnp.int32, (A_BLOCK, rows), 0)
    tok = lax.broadcasted_iota(jnp.int32, (A_BLOCK, rows), 1) & (A_BLOCK - 1)
    cb_s[...] = jnp.where(key <= tok, 0.0, NEG_INF)
    blk = lax.broadcasted_iota(jnp.int32, (N_BLOCKS_PROMPT, rows), 0)

    for i in range(N_BLOCKS_PROMPT):
        q2 = q_ref[0, i * A_BLOCK:(i + 1) * A_BLOCK, :]
        qb = jnp.concatenate([q2[:, g * A_HEAD_DIM:(g + 1) * A_HEAD_DIM] for g in range(A_GROUP)],
                             axis=0).astype(BF16)
        n_keys = (i + 1) * A_BLOCK
        s = _dot_nt(kb_s[:n_keys, :], qb) * A_SCALE
        s3 = s.reshape(n_keys // sub, sub, rows)
        pieces = []
        if i > 0:
            sg = _dot_nt(km, qb)
            rank = jnp.zeros(sg.shape, jnp.int32)
            for jp in range(i):
                row = sg[jp:jp + 1, :]
                beats = (row > sg) | ((row == sg) & (blk > jp))
                rank = rank + jnp.where(beats, 1, 0)
            gate_bias = jnp.where(rank < A_TOPK, 0.0, NEG_INF)
            for j in range(i):
                bias_j = jnp.broadcast_to(gate_bias[j:j + 1, :], (sub, rows))
                pieces.append(s3[j * tiles:(j + 1) * tiles] + bias_j[None])
        pieces.append(s3[i * tiles:] + cb_s[...].reshape(tiles, sub, rows))
        s3 = jnp.concatenate(pieces, axis=0)
        m = jnp.max(s3, axis=0)
        for shift in (4, 2, 1):
            m = jnp.maximum(m, pltpu.roll(m, shift, 0))
        p = jnp.exp(s3 - m[None]).reshape(n_keys, rows).astype(BF16)
        acc = _dot(vt_s[:, :n_keys], p)
        denom = acc[A_HEAD_DIM:A_HEAD_DIM + sub]
        out_t = (acc[:A_HEAD_DIM].reshape(A_HEAD_DIM // sub, sub, rows) / denom[None]).reshape(A_HEAD_DIM, rows)
        o_ref[0, i * A_BLOCK:(i + 1) * A_BLOCK, :] = jnp.concatenate(
            [out_t[:, g * A_BLOCK:(g + 1) * A_BLOCK].T for g in range(A_GROUP)], axis=1)


def _moba_prompt(qa, ka, va):
    gw = A_GROUP * A_HEAD_DIM
    kv_rows = SEQ * A_KV_HEADS
    return pl.pallas_call(
        _moba_prompt_kernel,
        grid=(BATCH, A_KV_HEADS),
        in_specs=[pl.BlockSpec((1, SEQ, gw), lambda b, g: (b, 0, g)),
                  pl.BlockSpec((1, kv_rows, A_HEAD_DIM), lambda b, g: (b, 0, 0)),
                  pl.BlockSpec((1, kv_rows, A_HEAD_DIM), lambda b, g: (b, 0, 0))],
        out_specs=pl.BlockSpec((1, SEQ, gw), lambda b, g: (b, 0, g)),
        out_shape=jax.ShapeDtypeStruct((BATCH, SEQ, A_WIDTH), F32),
        scratch_shapes=[pltpu.VMEM((SEQ, A_HEAD_DIM), BF16), pltpu.VMEM((VT_ROWS, SEQ), BF16),
                        pltpu.VMEM((A_BLOCK, A_GROUP * A_BLOCK), F32)],
        compiler_params=_params(("arbitrary", "arbitrary")),
        name="moba_prompt",
    )(qa, ka, va)


MLA_TQ = 128
MLA_TK = 256
MLA_ROW_CHUNK = 256


def _lanes2(x):
    return jnp.concatenate([x, x], axis=1)


def _mla_prompt_kernel(ql_ref, qr_ref, ckv_ref, kr_ref, wuv_ref, o_ref,
                       ckvb_s, krb_s, qls_s, qrs_s, m_s, l_s, acc_s):
    i = pl.program_id(1)
    tq = MLA_TQ
    rc = MLA_ROW_CHUNK
    chunks = [slice(c * rc, (c + 1) * rc) for c in range(B_HEADS * tq // rc)]

    @pl.when(i == 0)
    def _():
        ckvb_s[...] = ckv_ref[0].astype(BF16)
        krb_s[...] = kr_ref[0].astype(BF16)

    ql = ql_ref[0]
    qr = qr_ref[0]
    for h in range(B_HEADS):
        qls_s[h * tq:(h + 1) * tq, :] = ql[:, h * B_KV_LORA:(h + 1) * B_KV_LORA].astype(BF16)
        qrs_s[h * tq:(h + 1) * tq, :] = qr[:, h * B_ROPE:(h + 1) * B_ROPE].astype(BF16)

    def scores(rs, ckv_blk, kr_blk):
        return (_dot_nt(qls_s[rs, :], ckv_blk) + _dot_nt(qrs_s[rs, :], kr_blk)) * B_SCALE

    jd = (i * tq) // MLA_TK
    dstart = pl.multiple_of(jd * MLA_TK, MLA_TK)
    ckv_d = ckvb_s[pl.ds(dstart, MLA_TK), :]
    kr_d = krb_s[pl.ds(dstart, MLA_TK), :]
    sds = [scores(rs, ckv_d, kr_d) for rs in chunks]
    pds = []
    for c, (rs, s) in enumerate(zip(chunks, sds)):
        qpos = i * tq + ((c * rc + lax.broadcasted_iota(jnp.int32, s.shape, 0)) & (tq - 1))
        kpos = jd * MLA_TK + lax.broadcasted_iota(jnp.int32, s.shape, 1)
        s = jnp.where(qpos >= kpos, s, NEG_INF)
        m = jnp.broadcast_to(jnp.max(s, axis=1, keepdims=True), (rc, LANES))
        p = jnp.exp(s - _lanes2(m))
        m_s[rs, :] = m
        l_s[rs, :] = jnp.broadcast_to(jnp.sum(p, axis=1, keepdims=True), (rc, LANES))
        pds.append(p.astype(BF16))
    for rs, p in zip(chunks, pds):
        acc_s[rs, :] = _dot(p, ckv_d)

    def body(j, carry):
        start = pl.multiple_of(j * MLA_TK, MLA_TK)
        ckv_j = ckvb_s[pl.ds(start, MLA_TK), :]
        kr_j = krb_s[pl.ds(start, MLA_TK), :]
        sjs = [scores(rs, ckv_j, kr_j) for rs in chunks]
        pjs, alphas = [], []
        for rs, sj in zip(chunks, sjs):
            m_old = m_s[rs, :]
            m_new = jnp.maximum(m_old, jnp.max(sj, axis=1, keepdims=True))
            alpha = jnp.exp(m_old - m_new)
            pj = jnp.exp(sj - _lanes2(m_new))
            l_s[rs, :] = alpha * l_s[rs, :] + jnp.sum(pj, axis=1, keepdims=True)
            m_s[rs, :] = m_new
            pjs.append(pj.astype(BF16))
            alphas.append(alpha)
        for rs, pj, alpha in zip(chunks, pjs, alphas):
            acc_s[rs, :] = _lanes2(alpha) * acc_s[rs, :] + _dot(pj, ckv_j)
        return carry

    lax.fori_loop(0, jd, body, 0)

    out = (acc_s[...] / _lanes2(l_s[...])).astype(BF16)
    o_ref[0] = jnp.concatenate([_dot(out[h * tq:(h + 1) * tq], wuv_ref[h]) for h in range(B_HEADS)], axis=1)


def _mla_prompt(q_lat, q_rope, ckv, kr, wuv_t):
    tq = MLA_TQ
    rows = B_HEADS * tq
    return pl.pallas_call(
        _mla_prompt_kernel,
        grid=(BATCH, SEQ // tq),
        in_specs=[pl.BlockSpec((1, tq, B_HEADS * B_KV_LORA), lambda b, i: (b, i, 0)),
                  pl.BlockSpec((1, tq, B_HEADS * B_ROPE), lambda b, i: (b, i, 0)),
                  pl.BlockSpec((1, SEQ, B_KV_LORA), lambda b, i: (b, 0, 0)),
                  pl.BlockSpec((1, SEQ, B_ROPE), lambda b, i: (b, 0, 0)),
                  _resident(wuv_t.shape)],
        out_specs=pl.BlockSpec((1, tq, B_WIDTH), lambda b, i: (b, i, 0)),
        out_shape=jax.ShapeDtypeStruct((BATCH, SEQ, B_WIDTH), F32),
        scratch_shapes=[pltpu.VMEM((SEQ, B_KV_LORA), BF16), pltpu.VMEM((SEQ, B_ROPE), BF16),
                        pltpu.VMEM((rows, B_KV_LORA), BF16), pltpu.VMEM((rows, B_ROPE), BF16),
                        pltpu.VMEM((rows, LANES), F32), pltpu.VMEM((rows, LANES), F32),
                        pltpu.VMEM((rows, B_KV_LORA), F32)],
        compiler_params=_params(("arbitrary", "arbitrary")),
        name="mla_prompt",
    )(q_lat, q_rope, ckv, kr, wuv_t)


SAMPLE_ROWS = A_HEADS * DEC_SEQ
KV_ROWS = A_GROUP * DEC_SEQ


def _stack_heads(x, n_heads, width):
    return jnp.concatenate([x[:, h * width:(h + 1) * width] for h in range(n_heads)], axis=0)


def _unstack_heads(x, n_heads, tokens):
    return jnp.concatenate([x[h * tokens:(h + 1) * tokens] for h in range(n_heads)], axis=1)


def _causal_new_tokens(shape):
    tq = lax.broadcasted_iota(jnp.int32, shape, 0) & (DEC_SEQ - 1)
    tk = lax.broadcasted_iota(jnp.int32, shape, 1)
    return tk <= tq


def _log2(n):
    assert n & (n - 1) == 0
    return n.bit_length() - 1


def _set_lane(acc, lane, idx, col):
    return jnp.where(lane == idx, col, acc)


MOBA_GROUP = 4
MOBA_GROUPS = N_BLOCKS_PAST // MOBA_GROUP
PAGE_ROWS = PAGE_SIZE * A_KV_HEADS
BLOCK_ROWS = A_BLOCK * A_KV_HEADS
RING_SLOTS = 4
RING_LOOKAHEAD = RING_SLOTS - 1


def _moba_sample_kernel(pt_ref, q_ref, kn_ref, vn_ref, ck_ref, cv_ref, o_ref, kbuf, vbuf, sems, opart_s):
    n = pl.program_id(0)
    total = pl.num_programs(0) * MOBA_GROUPS

    def group_copies(seq, grp, half):
        copies = []
        for u in range(MOBA_GROUP * PAGES_PER_BLOCK):
            page = pt_ref[seq, MOBA_GROUP * PAGES_PER_BLOCK * grp + u]
            rows = pl.ds(u * PAGE_ROWS, PAGE_ROWS)
            copies.append((pltpu.make_async_copy(ck_ref.at[page], kbuf.at[half, rows], sems.at[0, half]), 0))
            copies.append((pltpu.make_async_copy(cv_ref.at[page], vbuf.at[half, rows], sems.at[1, half]), 1))
        return copies

    @pl.when(n == 0)
    def _():
        for g0 in range(RING_LOOKAHEAD):
            for cp, prio in group_copies(0, g0, g0 % RING_SLOTS):
                cp.start(priority=prio)

    qf = _stack_heads(q_ref[0], A_HEADS, A_HEAD_DIM)
    qb = qf.astype(BF16)
    row_head = lax.shift_right_logical(lax.broadcasted_iota(jnp.int32, (SAMPLE_ROWS, BLOCK_ROWS), 0),
                                       _log2(KV_ROWS))
    col_head = lax.broadcasted_iota(jnp.int32, (SAMPLE_ROWS, BLOCK_ROWS), 1) & (A_KV_HEADS - 1)
    same_head = row_head == col_head
    lane = lax.broadcasted_iota(jnp.int32, (SAMPLE_ROWS, LANES), 1)

    def group_partials(j0, half, stats):
        m_all, l_all, sg_all = stats
        block_rows = [slice(b * BLOCK_ROWS, (b + 1) * BLOCK_ROWS) for b in range(MOBA_GROUP)]
        scores, gates = [], []
        for rows in block_rows:
            kf = kbuf[half, rows, :]
            ks = jnp.sum(kf.reshape(BLOCK_ROWS // 8, 8, A_HEAD_DIM), axis=0)
            km = (ks[:A_KV_HEADS] + ks[A_KV_HEADS:]) * (1.0 / A_BLOCK)
            km_rows = jnp.concatenate(
                [jnp.broadcast_to(km[g:g + 1], (KV_ROWS, A_HEAD_DIM)) for g in range(A_KV_HEADS)], axis=0)
            gates.append(jnp.sum(qf * km_rows, axis=1, keepdims=True))
            scores.append(_dot_nt(qb, kf.astype(BF16)))
        probs = []
        for b in range(MOBA_GROUP):
            s = jnp.where(same_head, scores[b] * A_SCALE, NEG_INF)
            m = jnp.max(s, axis=1, keepdims=True)
            p = jnp.exp(s - m)
            probs.append(p.astype(BF16))
            m_all = _set_lane(m_all, lane, j0 + b, m)
            l_all = _set_lane(l_all, lane, j0 + b, jnp.sum(p, axis=1, keepdims=True))
            sg_all = _set_lane(sg_all, lane, j0 + b, gates[b])
        for b, rows in enumerate(block_rows):
            opart_s[j0 + b] = _dot(probs[b], vbuf[half, rows, :].astype(BF16))
        return m_all, l_all, sg_all

    def ring_step(jj, stats):
        for slot in range(RING_SLOTS):
            grp = jj * RING_SLOTS + slot
            nxt = n * MOBA_GROUPS + grp + RING_LOOKAHEAD

            @pl.when(nxt < total)
            def _():
                for cp, prio in group_copies(lax.shift_right_logical(nxt, _log2(MOBA_GROUPS)),
                                             nxt & (MOBA_GROUPS - 1), (slot + RING_LOOKAHEAD) % RING_SLOTS):
                    cp.start(priority=prio)

            for cp, _ in group_copies(n, grp, slot):
                cp.wait()
            stats = group_partials(grp * MOBA_GROUP, slot, stats)
        return stats

    zeros = jnp.zeros((SAMPLE_ROWS, LANES), F32)
    mm, ll, sg = lax.fori_loop(0, MOBA_GROUPS // RING_SLOTS, ring_step, (zeros, zeros, zeros))

    rank = jnp.zeros(sg.shape, jnp.int32)
    for jp in range(N_BLOCKS_PAST):
        col = sg[:, jp:jp + 1]
        beats = (col > sg) | ((col == sg) & (lane > jp))
        rank = rank + jnp.where(beats, 1, 0)
    sel = (rank < A_TOPK) & (lane < N_BLOCKS_PAST)

    mo, lo, oo = [], [], []
    for g in range(A_KV_HEADS):
        rs = slice(g * KV_ROWS, (g + 1) * KV_ROWS)
        kn = kn_ref[0, pl.ds(g, DEC_SEQ, stride=A_KV_HEADS), :]
        vn = vn_ref[0, pl.ds(g, DEC_SEQ, stride=A_KV_HEADS), :]
        s = _dot_nt(qb[rs], kn.astype(BF16)) * A_SCALE
        s = jnp.where(_causal_new_tokens(s.shape), s, NEG_INF)
        m = jnp.max(s, axis=1, keepdims=True)
        p = jnp.exp(s - m)
        mo.append(m)
        lo.append(jnp.sum(p, axis=1, keepdims=True))
        oo.append(_dot(p.astype(BF16), vn.astype(BF16)))
    m_own = jnp.concatenate(mo, axis=0)
    l_own = jnp.concatenate(lo, axis=0)
    o_own = jnp.concatenate(oo, axis=0)

    m_all = jnp.maximum(m_own, jnp.max(jnp.where(sel, mm, NEG_INF), axis=1, keepdims=True))
    w = jnp.where(sel, jnp.exp(mm - m_all), 0.0)
    w_own = jnp.exp(m_own - m_all)
    denom = jnp.sum(w * ll, axis=1, keepdims=True) + w_own * l_own
    acc = w_own * o_own
    for jb in range(N_BLOCKS_PAST):
        acc = acc + w[:, jb:jb + 1] * opart_s[jb]
    o_ref[0] = _unstack_heads(acc / denom, A_HEADS, DEC_SEQ)


def _moba_sample(page_table, qa, ka_new, va_new, cache_k2, cache_v2):
    assert MOBA_GROUPS % RING_SLOTS == 0

    def seq_spec(width, rows=DEC_SEQ):
        return pl.BlockSpec((1, rows, width), lambda n, pt: (n, 0, 0))

    new_kv_spec = seq_spec(A_HEAD_DIM, DEC_SEQ * A_KV_HEADS)
    grid_spec = pltpu.PrefetchScalarGridSpec(
        num_scalar_prefetch=1,
        grid=(DEC_BATCH,),
        in_specs=[seq_spec(A_WIDTH), new_kv_spec, new_kv_spec,
                  pl.BlockSpec(memory_space=pl.ANY), pl.BlockSpec(memory_space=pl.ANY)],
        out_specs=seq_spec(A_WIDTH),
        scratch_shapes=[pltpu.VMEM((RING_SLOTS, MOBA_GROUP * BLOCK_ROWS, A_HEAD_DIM), F32),
                        pltpu.VMEM((RING_SLOTS, MOBA_GROUP * BLOCK_ROWS, A_HEAD_DIM), F32),
                        pltpu.SemaphoreType.DMA((2, RING_SLOTS)),
                        pltpu.VMEM((N_BLOCKS_PAST, SAMPLE_ROWS, A_HEAD_DIM), F32)])
    return pl.pallas_call(
        _moba_sample_kernel,
        grid_spec=grid_spec,
        out_shape=jax.ShapeDtypeStruct((DEC_BATCH, DEC_SEQ, A_WIDTH), F32),
        compiler_params=_params(("arbitrary",)),
        name="moba_sample",
    )(page_table, qa, ka_new, va_new, cache_k2, cache_v2)


MLA_CHUNK_PAGES = 8
MLA_CHUNK_KEYS = MLA_CHUNK_PAGES * PAGE_SIZE
MLA_CHUNKS = N_PAGES // MLA_CHUNK_PAGES
MLA_SAMPLE_ROWS = B_HEADS * DEC_SEQ
MLA_SEQ_SLOTS = 2


def _mla_sample_kernel(pt_ref, ql_ref, qr_ref, cn_ref, rn_ref, wuv_ref, cc_ref, cr_ref, o_ref,
                       cbuf, rbuf, sems, opart_s):
    n = pl.program_id(0)
    slot = n & (MLA_SEQ_SLOTS - 1)

    def seq_copies(seq, slot):
        copies = []
        for u in range(N_PAGES):
            page = pt_ref[seq, u]
            copies.append((pltpu.make_async_copy(cc_ref.at[page], cbuf.at[slot, pl.ds(u * PAGE_SIZE, PAGE_SIZE)],
                                                 sems.at[0, slot]), u % 2))
            copies.append((pltpu.make_async_copy(cr_ref.at[page], rbuf.at[slot, u], sems.at[1, slot]), (u + 1) % 2))
        return copies

    @pl.when(n == 0)
    def _():
        for cp, prio in seq_copies(0, 0):
            cp.start(priority=prio)

    @pl.when(n + 1 < pl.num_programs(0))
    def _():
        for cp, prio in seq_copies(n + 1, (n + 1) & (MLA_SEQ_SLOTS - 1)):
            cp.start(priority=prio)

    qls = _stack_heads(ql_ref[0], B_HEADS, B_KV_LORA).astype(BF16)
    qrs = _stack_heads(qr_ref[0], B_HEADS, B_ROPE).astype(BF16)
    lane = lax.broadcasted_iota(jnp.int32, (MLA_SAMPLE_ROWS, LANES), 1)

    for cp, _ in seq_copies(n, slot):
        cp.wait()

    mm = jnp.zeros((MLA_SAMPLE_ROWS, LANES), F32)
    ll = jnp.zeros((MLA_SAMPLE_ROWS, LANES), F32)
    latents, scores = [], []
    for k in range(MLA_CHUNKS):
        cb = cbuf[slot, k * MLA_CHUNK_KEYS:(k + 1) * MLA_CHUNK_KEYS, :].astype(BF16)
        rt = jnp.concatenate([rbuf[slot, k * MLA_CHUNK_PAGES + u] for u in range(MLA_CHUNK_PAGES)],
                             axis=1).astype(BF16)
        latents.append(cb)
        scores.append(_dot_nt(qls, cb) + _dot(qrs, rt))
    probs = []
    for k in range(MLA_CHUNKS):
        s = scores[k] * B_SCALE
        m = jnp.max(s, axis=1, keepdims=True)
        p = jnp.exp(s - m)
        probs.append(p.astype(BF16))
        mm = _set_lane(mm, lane, k, m)
        ll = _set_lane(ll, lane, k, jnp.sum(p, axis=1, keepdims=True))
    for k in range(MLA_CHUNKS):
        opart_s[k] = _dot(probs[k], latents[k])

    cn = cn_ref[0].astype(BF16)
    sn = (_dot_nt(qls, cn) + _dot_nt(qrs, rn_ref[0].astype(BF16))) * B_SCALE
    sn = jnp.where(_causal_new_tokens(sn.shape), sn, NEG_INF)
    m_own = jnp.max(sn, axis=1, keepdims=True)
    p_own = jnp.exp(sn - m_own)
    l_own = jnp.sum(p_own, axis=1, keepdims=True)
    o_own = _dot(p_own.astype(BF16), cn)

    cached = lane < MLA_CHUNKS
    m_all = jnp.maximum(m_own, jnp.max(jnp.where(cached, mm, NEG_INF), axis=1, keepdims=True))
    w = jnp.where(cached, jnp.exp(mm - m_all), 0.0)
    w_own = jnp.exp(m_own - m_all)
    denom = jnp.sum(w * ll, axis=1, keepdims=True) + w_own * l_own
    acc = w_own * o_own
    for c in range(MLA_CHUNKS):
        acc = acc + w[:, c:c + 1] * opart_s[c]
    out = (acc / denom).astype(BF16)
    o_ref[0] = jnp.concatenate(
        [_dot(out[h * DEC_SEQ:(h + 1) * DEC_SEQ], wuv_ref[h]) for h in range(B_HEADS)], axis=1)


def _mla_sample(page_table, q_lat, q_rope, ckv_new, kr_new, wuv_t, cache_ckv2, cache_krt):
    assert MLA_SEQ_SLOTS & (MLA_SEQ_SLOTS - 1) == 0 and MLA_CHUNKS <= LANES

    def seq_spec(width):
        return pl.BlockSpec((1, DEC_SEQ, width), lambda n, pt: (n, 0, 0))

    grid_spec = pltpu.PrefetchScalarGridSpec(
        num_scalar_prefetch=1,
        grid=(DEC_BATCH,),
        in_specs=[seq_spec(B_HEADS * B_KV_LORA), seq_spec(B_HEADS * B_ROPE), seq_spec(B_KV_LORA),
                  seq_spec(B_ROPE),
                  pl.BlockSpec(wuv_t.shape, lambda n, pt: (0, 0, 0), pipeline_mode=pl.Buffered(1)),
                  pl.BlockSpec(memory_space=pl.ANY), pl.BlockSpec(memory_space=pl.ANY)],
        out_specs=seq_spec(B_WIDTH),
        scratch_shapes=[pltpu.VMEM((MLA_SEQ_SLOTS, PAST_LEN, B_KV_LORA), F32),
                        pltpu.VMEM((MLA_SEQ_SLOTS, N_PAGES, B_ROPE, PAGE_SIZE), F32),
                        pltpu.SemaphoreType.DMA((2, MLA_SEQ_SLOTS)),
                        pltpu.VMEM((MLA_CHUNKS, MLA_SAMPLE_ROWS, B_KV_LORA), F32)])
    return pl.pallas_call(
        _mla_sample_kernel,
        grid_spec=grid_spec,
        out_shape=jax.ShapeDtypeStruct((DEC_BATCH, DEC_SEQ, B_WIDTH), F32),
        compiler_params=_params(("arbitrary",)),
        name="mla_sample",
    )(page_table, q_lat, q_rope, ckv_new, kr_new, wuv_t, cache_ckv2, cache_krt)


def _out_proj_kernel(oa_ref, ga_ref, ob_ref, gb_ref, x_ref, gate_ref, w_ref, lng_ref, lnb_ref, y_ref,
                     *, per_row_mod):
    u = jnp.concatenate([oa_ref[...] * _silu(ga_ref[...]), ob_ref[...] * _silu(gb_ref[...])], axis=1)
    o = _dot(u.astype(BF16), w_ref[...])
    gate = gate_ref[...] if per_row_mod else gate_ref[0]
    r = ALPHA * x_ref[...] + gate * o
    mu = jnp.mean(r, axis=-1, keepdims=True)
    d = r - mu
    var = jnp.mean(d * d, axis=-1, keepdims=True)
    y_ref[...] = d * lax.rsqrt(var + LN_EPS) * lng_ref[...] + lnb_ref[...]


def _out_proj(oa, ga, ob, gb, x, mod, w_out, ln_g, ln_b, *, per_row_mod, rows_per_batch):
    rows = x.shape[0]
    tm = 256
    if per_row_mod:
        gate_spec = pl.BlockSpec((tm, D_MODEL), lambda i: (i, 2))
    else:
        tiles_per_batch = rows_per_batch // tm
        gate_spec = pl.BlockSpec((1, 1, D_MODEL), lambda i: (i // tiles_per_batch, 0, 2))
    half = pl.BlockSpec((tm, A_WIDTH), lambda i: (i, 0))
    full = pl.BlockSpec((tm, D_MODEL), lambda i: (i, 0))
    return pl.pallas_call(
        functools.partial(_out_proj_kernel, per_row_mod=per_row_mod),
        grid=(rows // tm,),
        in_specs=[half, half, half, half, full, gate_spec,
                  _resident(w_out.shape), _resident(ln_g.shape), _resident(ln_b.shape)],
        out_specs=full,
        out_shape=jax.ShapeDtypeStruct((rows, D_MODEL), F32),
        compiler_params=_params(("arbitrary",)),
        name="out_proj_sample" if per_row_mod else "out_proj_prompt",
    )(oa, ga, ob, gb, x, mod, w_out, ln_g, ln_b)


def _rope_tables(pos, half):
    inv = jnp.power(ROPE_THETA, -jnp.arange(half, dtype=F32) / half)
    ang = pos.astype(F32)[:, None] * inv[None, :]
    cos, sin = jnp.cos(ang), jnp.sin(ang)
    return jnp.concatenate([cos, cos], axis=1), jnp.concatenate([-sin, sin], axis=1)


def _rope_tables_128(pos):
    cosa, sina = _rope_tables(pos, A_HEAD_DIM // 2)
    cosb, sinb = _rope_tables(pos, B_ROPE // 2)
    return cosa, sina, jnp.tile(cosb, (1, 2)), jnp.tile(sinb, (1, 2))


def kernel(x_prompt, x_sample, cache_k, cache_v, cache_ckv, cache_kr, page_table, c_prompt, c_sample,
           w_ada, b_ada, w_in, q_norm_g, w_uq, kv_norm_g, w_uk, w_uv, w_out, ln_g, ln_b):
    layer = 0
    n_pool = cache_k.shape[1]
    rows_p = BATCH * SEQ
    rows_s = DEC_BATCH * DEC_SEQ

    w_p = jnp.pad(w_in[layer].astype(BF16), ((0, 0), (0, W_IN_COLS - IN_TOTAL)))
    wuq = w_uq[layer].reshape(B_Q_LORA, B_HEADS, B_NOPE + B_ROPE)
    wuq_n = wuq[:, :, :B_NOPE].reshape(B_Q_LORA, B_HEADS * B_NOPE).astype(BF16)
    wuq_r = wuq[:, :, B_NOPE:].reshape(B_Q_LORA, B_HEADS * B_ROPE).astype(BF16)
    wuk_t = jnp.transpose(w_uk[layer], (1, 2, 0)).astype(BF16)
    wuv_t = jnp.transpose(w_uv[layer], (1, 0, 2)).astype(BF16)
    w_o = w_out[layer].astype(BF16)
    qg, kvg, lng, lnb = q_norm_g[layer][None], kv_norm_g[layer][None], ln_g[layer][None], ln_b[layer][None]

    n_c = BATCH + DEC_BATCH
    pad = (-n_c) % 8
    c_all = jnp.concatenate([c_prompt, c_sample, jnp.zeros((pad, D_MODEL), F32)], axis=0)
    mod = _ada_mod(c_all, w_ada[layer], b_ada[layer][None])
    mod_p = mod[:BATCH].reshape(BATCH, 1, 3 * D_MODEL)
    mod_s = jnp.repeat(mod[BATCH:n_c], DEC_SEQ, axis=0)

    tabs_p = _rope_tables_128(jnp.arange(SEQ, dtype=jnp.int32))
    tm = 256
    tabs_s = tuple(jnp.tile(t, (tm // DEC_SEQ, 1))
                   for t in _rope_tables_128(PAST_LEN + jnp.arange(DEC_SEQ, dtype=jnp.int32)))

    xp = x_prompt.reshape(rows_p, D_MODEL)
    xs = x_sample.reshape(rows_s, D_MODEL)
    proj_w = (w_p, wuq_n, wuq_r, wuk_t, qg, kvg)
    qa_p, ka_p, va_p, ga_p, ql_p, qr_p, ckv_p, kr_p, gb_p = _in_proj(
        xp, mod_p, *proj_w, tabs_p, per_row_mod=False, rows_per_batch=SEQ)
    qa_s, ka_s, va_s, ga_s, ql_s, qr_s, ckv_s, kr_s, gb_s = _in_proj(
        xs, mod_s, *proj_w, tabs_s, per_row_mod=True, rows_per_batch=DEC_SEQ)

    oa_p = _moba_prompt(qa_p.reshape(BATCH, SEQ, A_WIDTH), ka_p.reshape(BATCH, SEQ * A_KV_HEADS, A_HEAD_DIM),
                        va_p.reshape(BATCH, SEQ * A_KV_HEADS, A_HEAD_DIM))
    ob_p = _mla_prompt(ql_p.reshape(BATCH, SEQ, -1), qr_p.reshape(BATCH, SEQ, -1),
                       ckv_p.reshape(BATCH, SEQ, B_KV_LORA), kr_p.reshape(BATCH, SEQ, B_ROPE), wuv_t)
    y_p = _out_proj(oa_p.reshape(rows_p, A_WIDTH), ga_p, ob_p.reshape(rows_p, B_WIDTH), gb_p, xp, mod_p,
                    w_o, lng, lnb, per_row_mod=False, rows_per_batch=SEQ)

    cache_k2 = cache_k[layer].reshape(n_pool, PAGE_ROWS, A_HEAD_DIM)
    cache_v2 = cache_v[layer].reshape(n_pool, PAGE_ROWS, A_HEAD_DIM)
    cache_krt = jnp.swapaxes(cache_kr[layer], 1, 2)
    oa_s = _moba_sample(page_table, qa_s.reshape(DEC_BATCH, DEC_SEQ, A_WIDTH),
                        ka_s.reshape(DEC_BATCH, DEC_SEQ * A_KV_HEADS, A_HEAD_DIM),
                        va_s.reshape(DEC_BATCH, DEC_SEQ * A_KV_HEADS, A_HEAD_DIM), cache_k2, cache_v2)
    ob_s = _mla_sample(page_table, ql_s.reshape(DEC_BATCH, DEC_SEQ, -1), qr_s.reshape(DEC_BATCH, DEC_SEQ, -1),
                       ckv_s.reshape(DEC_BATCH, DEC_SEQ, B_KV_LORA), kr_s.reshape(DEC_BATCH, DEC_SEQ, B_ROPE),
                       wuv_t, cache_ckv[layer], cache_krt)
    y_s = _out_proj(oa_s.reshape(rows_s, A_WIDTH), ga_s, ob_s.reshape(rows_s, B_WIDTH), gb_s, xs, mod_s,
                    w_o, lng, lnb, per_row_mod=True, rows_per_batch=DEC_SEQ)

    return (y_p.reshape(BATCH, SEQ, D_MODEL), y_s.reshape(DEC_BATCH, DEC_SEQ, D_MODEL),
            ka_p.reshape(DEPTH, BATCH, SEQ, A_KV_HEADS, A_HEAD_DIM),
            va_p.reshape(DEPTH, BATCH, SEQ, A_KV_HEADS, A_HEAD_DIM),
            ckv_p.reshape(DEPTH, BATCH, SEQ, B_KV_LORA), kr_p.reshape(DEPTH, BATCH, SEQ, B_ROPE),
            ka_s.reshape(DEPTH, DEC_BATCH, DEC_SEQ, A_KV_HEADS, A_HEAD_DIM),
            va_s.reshape(DEPTH, DEC_BATCH, DEC_SEQ, A_KV_HEADS, A_HEAD_DIM),
            ckv_s.reshape(DEPTH, DEC_BATCH, DEC_SEQ, B_KV_LORA), kr_s.reshape(DEPTH, DEC_BATCH, DEC_SEQ, B_ROPE))
```
